```python
import math
import jax, jax.numpy as jnp
from jax import lax
import numpy as np

D_MODEL = 4096
BATCH = 8
SEQ = 2048
DEPTH = 2
DEC_BATCH = 2
DEC_SEQ = 4096
PAST_LEN = 128

SSM_WIDTH = D_MODEL
SSM_HEAD_DIM = 64
SSM_HEADS = SSM_WIDTH // SSM_HEAD_DIM
SSM_GROUPS = 8
SSM_STATE = 128
SSM_CONV = 5
SSD_CHUNK = 128
SSM_GN = SSM_GROUPS * SSM_STATE
SSM_XBC = SSM_WIDTH + 2 * SSM_GN
SSM_IN = SSM_WIDTH + SSM_XBC + 2 * SSM_HEADS
HY_WIDTH = D_MODEL // 2
HY_SHORT = 3
HY_EMB = 33
HY_ORDER = 64
HY_FAST_DECAY = 0.3
HY_SLOW_DECAY = 1.5
HY_TARGET = 1e-2
HY_IN = 3 * HY_WIDTH
SG_WIDTH = D_MODEL // 2
SG_CHUNK = 128
SG_GROUPS = 16
SG_GROUP_DIM = SG_WIDTH // SG_GROUPS
SG_IN = 2 * SG_WIDTH
N_BRANCH = 3
GATE_IN = N_BRANCH * D_MODEL
MIX_WIDTH = SSM_WIDTH + HY_WIDTH + SG_WIDTH
N_IN = SSM_IN + HY_IN + SG_IN + GATE_IN
D_FF = 256 * math.ceil(8 * D_MODEL / 3 / 256)
FFN_CONV = 3
RMS_EPS = 1e-6
LN_EPS = 1e-5
SPLIT_IN = (SSM_WIDTH, SSM_WIDTH + SSM_XBC, SSM_IN, SSM_IN + HY_IN, SSM_IN + HY_IN + SG_IN)

kernel_name = "hybrid_ssd_hyena_sgu_encoder"


def _rmsnorm(x, g):
    xf = x.astype(jnp.float32)
    y = xf * lax.rsqrt(jnp.mean(xf * xf, axis=-1, keepdims=True) + RMS_EPS)
    return (y * g.astype(jnp.float32)).astype(x.dtype)


def _dwconv(x, w, b):
    k = w.shape[0]
    pad = k // 2
    n = x.shape[1]
    xp = jnp.pad(x, ((0, 0), (pad, pad), (0, 0)))
    out = b
    for j in range(k):
        out = out + w[j] * xp[:, j:j + n]
    return out


def _ssd(x, dt, a, bm, cm):
    bt, n, nh, hp = x.shape
    ng, ns = bm.shape[2], bm.shape[3]
    hg = nh // ng
    nc = n // SSD_CHUNK
    t = SSD_CHUNK
    xc = (x * dt[..., None]).reshape(bt, nc, t, ng, hg, hp)
    a_cs = jnp.cumsum((dt * a).reshape(bt, nc, t, ng, hg), axis=2)
    bc = bm.reshape(bt, nc, t, ng, ns)
    cc = cm.reshape(bt, nc, t, ng, ns)
    lower = jnp.tril(jnp.ones((t, t), dtype=bool))[None, None, :, :, None, None]
    seg = a_cs[:, :, :, None] - a_cs[:, :, None, :]
    decay = jnp.exp(jnp.where(lower, seg, -jnp.inf))
    cb = jnp.einsum('bclgn,bcsgn->bclsg', cc, bc)
    y_diag = jnp.einsum('bclsg,bclsgh,bcsghp->bclghp', cb, decay, xc)
    decay_end = jnp.exp(a_cs[:, :, -1:] - a_cs)
    states = jnp.einsum('bclgn,bclgh,bclghp->bcghpn', bc, decay_end, xc)
    chunk_decay = jnp.exp(a_cs[:, :, -1])

    def step(h, inp):
        s, d = inp
        return h * d[..., None, None] + s, h

    h0 = jnp.zeros((bt, ng, hg, hp, ns), x.dtype)
    _, h_in = lax.scan(step, h0, (jnp.moveaxis(states, 1, 0), jnp.moveaxis(chunk_decay, 1, 0)))
    h_in = jnp.moveaxis(h_in, 0, 1)
    y_off = jnp.einsum('bclgn,bcghpn,bclgh->bclghp', cc, h_in, jnp.exp(a_cs))
    return (y_diag + y_off).reshape(bt, n, nh, hp)


def _ssd_branch(z, xbc, dt_raw, conv_w, conv_b, dt_bias, a_log, d_skip, norm_g):
    f32 = jnp.float32
    xbc = jax.nn.silu(_dwconv(xbc.astype(f32), conv_w.astype(f32), conv_b.astype(f32)))
    bt, n, _ = xbc.shape
    xs = xbc[..., :SSM_WIDTH].reshape(bt, n, SSM_HEADS, SSM_HEAD_DIM)
    bm = xbc[..., SSM_WIDTH:SSM_WIDTH + SSM_GN].reshape(bt, n, SSM_GROUPS, SSM_STATE)
    cm = xbc[..., SSM_WIDTH + SSM_GN:].reshape(bt, n, SSM_GROUPS, SSM_STATE)
    dt = jax.nn.softplus(dt_raw.astype(f32).reshape(bt, n, 2, SSM_HEADS) + dt_bias.astype(f32))
    a = -jnp.exp(a_log.astype(f32))
    flip = lambda v: jnp.flip(v, axis=1)
    y = (_ssd(xs, dt[:, :, 0], a[0], bm, cm)
         + flip(_ssd(flip(xs), flip(dt[:, :, 1]), a[1], flip(bm), flip(cm)))
         + xs * d_skip.astype(f32)[:, None])
    y = y.reshape(bt, n, SSM_WIDTH) * jax.nn.silu(z.astype(f32))
    yg = y.reshape(bt, n, SSM_GROUPS, SSM_WIDTH // SSM_GROUPS)
    yg = yg * lax.rsqrt(jnp.mean(yg * yg, axis=-1, keepdims=True) + RMS_EPS)
    return yg.reshape(bt, n, SSM_WIDTH) * norm_g.astype(f32)


def _hyena_filters(n, w1, b1, w2, b2, w3, b3, freq, w4):
    f32 = jnp.float32
    t = jnp.linspace(0.0, 1.0, n, dtype=f32)[:, None]
    bands = (HY_EMB - 1) // 2
    wpos = (2.0 * math.pi / n) * jnp.arange(n, dtype=f32)[:, None]
    fr = jnp.linspace(1e-4, bands - 1, bands, dtype=f32)[None, :]
    zpos = jnp.concatenate([t, jnp.cos(fr * wpos), -jnp.sin(fr * wpos)], axis=-1)
    fq = freq.astype(f32)
    h = jnp.sin(fq * (zpos @ w1.astype(f32) + b1.astype(f32)))
    h = jnp.sin(fq * (h @ w2.astype(f32) + b2.astype(f32)))
    h = jnp.sin(fq * (h @ w3.astype(f32) + b3.astype(f32)))
    h = h @ w4.astype(f32)
    min_decay = math.log(HY_TARGET) / HY_SLOW_DECAY
    max_decay = math.log(HY_TARGET) / HY_FAST_DECAY
    deltas = jnp.abs(jnp.linspace(min_decay, max_decay, HY_WIDTH, dtype=f32))
    window = jnp.exp(-t * deltas[None, :])
    return h.reshape(n, 2, HY_WIDTH) * window[:, None, :]


def _hyena_branch(proj, conv_w, conv_b, w1, b1, w2, b2, w3, b3, freq, w4, bias):
    f32 = jnp.float32
    u = _dwconv(proj.astype(f32), conv_w.astype(f32), conv_b.astype(f32))
    x0 = u[..., :HY_WIDTH]
    x1 = u[..., HY_WIDTH:2 * HY_WIDTH]
    v = u[..., 2 * HY_WIDTH:]
    n = u.shape[1]
    k = _hyena_filters(n, w1, b1, w2, b2, w3, b3, freq, w4)
    k2 = jnp.concatenate([k[:, 0], jnp.zeros((1, HY_WIDTH), f32), k[:0:-1, 1]], axis=0)
    w = v * x1
    yc = jnp.fft.irfft(jnp.fft.rfft(w, n=2 * n, axis=1) * jnp.fft.rfft(k2, axis=0)[None],
                       n=2 * n, axis=1)[:, :n]
    return x0 * (yc + w * bias.astype(f32))


def _sgu_branch(proj, ln_g, ln_b, ws, bs):
    f32 = jnp.float32
    h = jax.nn.gelu(proj.astype(f32), approximate=False)
    u, v = h[..., :SG_WIDTH], h[..., SG_WIDTH:]
    mu = jnp.mean(v, axis=-1, keepdims=True)
    var = jnp.mean(jnp.square(v - mu), axis=-1, keepdims=True)
    v = (v - mu) * lax.rsqrt(var + LN_EPS) * ln_g.astype(f32) + ln_b.astype(f32)
    bt, n, _ = v.shape
    vc = v.reshape(bt, n // SG_CHUNK, SG_CHUNK, SG_GROUPS, SG_GROUP_DIM)
    mixed = jnp.einsum('gts,bcsgd->bctgd', ws.astype(f32), vc) + bs.astype(f32).T[:, :, None]
    return u * mixed.reshape(bt, n, SG_WIDTH)


def _layer(x, norm1_g, w_in, b_gate, ssm_conv_w, ssm_conv_b, ssm_dt_bias, ssm_a_log, ssm_d,
           ssm_norm_g, hy_conv_w, hy_conv_b, hy_w1, hy_b1, hy_w2, hy_b2, hy_w3, hy_b3, hy_freq,
           hy_w4, hy_bias, sg_ln_g, sg_ln_b, sg_ws, sg_bs, w_br, w_out, norm2_g, w_up,
           ffn_conv_w, ffn_conv_b, w_down):
    dtype = x.dtype
    h = _rmsnorm(x, norm1_g)
    p = h @ w_in
    z, xbc, dt_raw, hy_p, sg_p, gate_l = jnp.split(p, SPLIT_IN, axis=-1)
    y_m = _ssd_branch(z, xbc, dt_raw, ssm_conv_w, ssm_conv_b, ssm_dt_bias, ssm_a_log, ssm_d,
                      ssm_norm_g).astype(dtype)
    y_h = _hyena_branch(hy_p, hy_conv_w, hy_conv_b, hy_w1, hy_b1, hy_w2, hy_b2, hy_w3, hy_b3,
                        hy_freq, hy_w4, hy_bias).astype(dtype)
    y_g = _sgu_branch(sg_p, sg_ln_g, sg_ln_b, sg_ws, sg_bs).astype(dtype)
    bt, n, _ = x.shape
    gates = jax.nn.sigmoid((gate_l + b_gate).astype(jnp.float32)).astype(dtype)
    gates = gates.reshape(bt, n, N_BRANCH, D_MODEL)
    merged = (gates[:, :, 0] * (y_m @ w_br[:SSM_WIDTH])
              + gates[:, :, 1] * (y_h @ w_br[SSM_WIDTH:SSM_WIDTH + HY_WIDTH])
              + gates[:, :, 2] * (y_g @ w_br[SSM_WIDTH + HY_WIDTH:]))
    x = x + merged @ w_out
    h2 = _rmsnorm(x, norm2_g)
    up = _dwconv(h2 @ w_up, ffn_conv_w, ffn_conv_b)
    g, val = up[..., :D_FF], up[..., D_FF:]
    return x + (jax.nn.silu(g) * val) @ w_down


def setup_inputs(seed: int = 0) -> dict:
    key = jax.random.key(seed)
    ks = jax.random.split(key, 40)
    f32 = jnp.float32
    L = DEPTH

    def nrm(k, shape, s):
        return s * jax.random.normal(k, shape, f32)

    def gain(k, shape):
        return 1.0 + 0.01 * jax.random.normal(k, shape, f32)

    dt0 = jnp.exp(jax.random.uniform(ks[8], (L, 2, SSM_HEADS), f32,
                                     minval=math.log(1e-3), maxval=math.log(1e-1)))
    dt_bias = dt0 + jnp.log(-jnp.expm1(-dt0))
    a_log = jnp.log(jax.random.uniform(ks[9], (L, 2, SSM_HEADS), f32, minval=1.0, maxval=16.0))
    w_br = jnp.concatenate([nrm(ks[26], (L, SSM_WIDTH, D_MODEL), SSM_WIDTH ** -0.5),
                            nrm(ks[27], (L, HY_WIDTH, D_MODEL), HY_WIDTH ** -0.5),
                            nrm(ks[28], (L, SG_WIDTH, D_MODEL), SG_WIDTH ** -0.5)], axis=1)
    return {
        "x_prompt": jax.random.normal(ks[0], (BATCH, SEQ, D_MODEL), f32),
        "x_sample": jax.random.normal(ks[1], (DEC_BATCH, DEC_SEQ, D_MODEL), f32),
        "norm1_g": gain(ks[2], (L, D_MODEL)),
        "w_in": nrm(ks[3], (L, D_MODEL, N_IN), D_MODEL ** -0.5),
        "b_gate": nrm(ks[4], (L, GATE_IN), 0.01),
        "ssm_conv_w": nrm(ks[5], (L, SSM_CONV, SSM_XBC), SSM_CONV ** -0.5),
        "ssm_conv_b": nrm(ks[6], (L, SSM_XBC), 0.01),
        "ssm_dt_bias": dt_bias,
        "ssm_a_log": a_log,
        "ssm_d": gain(ks[7], (L, SSM_HEADS)),
        "ssm_norm_g": gain(ks[10], (L, SSM_WIDTH)),
        "hy_conv_w": nrm(ks[11], (L, HY_SHORT, HY_IN), HY_SHORT ** -0.5),
        "hy_conv_b": nrm(ks[12], (L, HY_IN), 0.01),
        "hy_w1": nrm(ks[13], (L, HY_EMB, HY_ORDER), HY_EMB ** -0.5),
        "hy_b1": nrm(ks[14], (L, HY_ORDER), 0.1),
        "hy_w2": nrm(ks[15], (L, HY_ORDER, HY_ORDER), HY_ORDER ** -0.5),
        "hy_b2": nrm(ks[16], (L, HY_ORDER), 0.1),
        "hy_w3": nrm(ks[17], (L, HY_ORDER, HY_ORDER), HY_ORDER ** -0.5),
        "hy_b3": nrm(ks[18], (L, HY_ORDER), 0.1),
        "hy_freq": gain(ks[19], (L, HY_ORDER)),
        "hy_w4": nrm(ks[20], (L, HY_ORDER, 2 * HY_WIDTH), 0.05 * HY_ORDER ** -0.5),
        "hy_bias": nrm(ks[21], (L, HY_WIDTH), 1.0),
        "sg_ln_g": gain(ks[22], (L, SG_WIDTH)),
        "sg_ln_b": nrm(ks[23], (L, SG_WIDTH), 0.01),
        "sg_ws": nrm(ks[24], (L, SG_GROUPS, SG_CHUNK, SG_CHUNK), SG_CHUNK ** -0.5),
        "sg_bs": gain(ks[25], (L, SG_GROUPS, SG_CHUNK)),
        "w_br": w_br,
        "w_out": nrm(ks[29], (L, D_MODEL, D_MODEL), D_MODEL ** -0.5),
        "norm2_g": gain(ks[30], (L, D_MODEL)),
        "w_up": nrm(ks[31], (L, D_MODEL, 2 * D_FF), D_MODEL ** -0.5),
        "ffn_conv_w": nrm(ks[32], (L, FFN_CONV, 2 * D_FF), FFN_CONV ** -0.5),
        "ffn_conv_b": nrm(ks[33], (L, 2 * D_FF), 0.01),
        "w_down": nrm(ks[34], (L, D_FF, D_MODEL), D_FF ** -0.5),
        "normf_g": gain(ks[35], (D_MODEL,)),
    }


def reference(x_prompt, x_sample, norm1_g, w_in, b_gate, ssm_conv_w, ssm_conv_b, ssm_dt_bias,
              ssm_a_log, ssm_d, ssm_norm_g, hy_conv_w, hy_conv_b, hy_w1, hy_b1, hy_w2, hy_b2,
              hy_w3, hy_b3, hy_freq, hy_w4, hy_bias, sg_ln_g, sg_ln_b, sg_ws, sg_bs, w_br, w_out,
              norm2_g, w_up, ffn_conv_w, ffn_conv_b, w_down, normf_g):
    def trunk(x):
        for l in range(DEPTH):
            x = _layer(x, norm1_g[l], w_in[l], b_gate[l], ssm_conv_w[l], ssm_conv_b[l],
                       ssm_dt_bias[l], ssm_a_log[l], ssm_d[l], ssm_norm_g[l], hy_conv_w[l],
                       hy_conv_b[l], hy_w1[l], hy_b1[l], hy_w2[l], hy_b2[l], hy_w3[l], hy_b3[l],
                       hy_freq[l], hy_w4[l], hy_bias[l], sg_ln_g[l], sg_ln_b[l], sg_ws[l], sg_bs[l],
                       w_br[l], w_out[l], norm2_g[l], w_up[l], ffn_conv_w[l], ffn_conv_b[l],
                       w_down[l])
        return _rmsnorm(x, normf_g)

    y_prompt = trunk(x_prompt)
    y_sample = trunk(x_sample)
    return (y_prompt, y_sample)
```

```python
import functools
import math

import jax
import jax.numpy as jnp
from jax import lax
from jax.experimental import pallas as pl
from jax.experimental.pallas import tpu as pltpu

F32 = jnp.float32
BF16 = jnp.bfloat16

D_MODEL = 4096
SSM_WIDTH = D_MODEL
SSM_HEAD_DIM = 64
SSM_HEADS = SSM_WIDTH // SSM_HEAD_DIM
SSM_GROUPS = 8
SSM_STATE = 128
SSM_CONV = 5
SSD_CHUNK = 128
SSM_GN = SSM_GROUPS * SSM_STATE
SSM_XBC = SSM_WIDTH + 2 * SSM_GN
SSM_IN = SSM_WIDTH + SSM_XBC + 2 * SSM_HEADS
SSM_GROUP_WIDTH = SSM_WIDTH // SSM_GROUPS
HY_WIDTH = D_MODEL // 2
HY_SHORT = 3
HY_EMB = 33
HY_BANDS = (HY_EMB - 1) // 2
HY_ORDER = 64
HY_FAST_DECAY = 0.3
HY_SLOW_DECAY = 1.5
HY_TARGET = 1e-2
HY_IN = 3 * HY_WIDTH
SG_WIDTH = D_MODEL // 2
SG_CHUNK = 128
SG_GROUPS = 16
SG_GROUP_DIM = SG_WIDTH // SG_GROUPS
SG_IN = 2 * SG_WIDTH
N_BRANCH = 3
GATE_IN = N_BRANCH * D_MODEL
D_FF = 256 * math.ceil(8 * D_MODEL / 3 / 256)
FFN_CONV = 3
RMS_EPS = 1e-6
LN_EPS = 1e-5

V7X_SCOPED_VMEM_BYTES = 56 * 1024 * 1024
LANES = 128
SUBLANES_BF16 = 16
HALO_KEEP = 8


def _tile(dim, pref):
    t = pref
    while dim % t:
        t //= 2
    return t


def _params(semantics):
    return pltpu.CompilerParams(dimension_semantics=semantics, vmem_limit_bytes=V7X_SCOPED_VMEM_BYTES)


def _silu(x):
    return x * jax.nn.sigmoid(x)


def _rmsnorm_kernel(x_ref, g_ref, o_ref):
    x = x_ref[...]
    ms = jnp.mean(x * x, axis=-1, keepdims=True)
    o_ref[...] = (x * lax.rsqrt(ms + RMS_EPS) * g_ref[...]).astype(o_ref.dtype)


def _rmsnorm(x, g, out_dtype):
    m, d = x.shape
    tr = _tile(m, 256)
    return pl.pallas_call(
        _rmsnorm_kernel,
        grid=(m // tr,),
        in_specs=[pl.BlockSpec((tr, d), lambda i: (i, 0)), pl.BlockSpec((1, d), lambda i: (0, 0))],
        out_specs=pl.BlockSpec((tr, d), lambda i: (i, 0)),
        out_shape=jax.ShapeDtypeStruct((m, d), out_dtype),
        compiler_params=_params(("parallel",)),
        name="rmsnorm",
    )(x, g.reshape(1, d).astype(F32))


def _mm_kernel(a_ref, b_ref, *rest, nk, n_extra, epilogue):
    extra = rest[:n_extra]
    o_ref = rest[n_extra]

    def finish(acc):
        if epilogue is not None:
            acc = epilogue(acc, *[r[...] for r in extra])
        o_ref[...] = acc.astype(o_ref.dtype)

    if nk == 1:
        finish(jnp.dot(a_ref[...], b_ref[...], preferred_element_type=F32))
    else:
        acc_ref = rest[n_extra + 1]
        k = pl.program_id(2)

        @pl.when(k == 0)
        def _():
            acc_ref[...] = jnp.zeros_like(acc_ref)

        acc_ref[...] += jnp.dot(a_ref[...], b_ref[...], preferred_element_type=F32)

        @pl.when(k == nk - 1)
        def _():
            finish(acc_ref[...])


def _matmul(a, b, *, out_dtype, name, epilogue=None, rows=(), tiles=(), tm=1024, tn=1024, nk=1):
    m, k = a.shape
    _, n = b.shape
    tm, tn, tk = _tile(m, tm), _tile(n, tn), k // nk
    kidx = (lambda kk: kk[0]) if nk > 1 else (lambda kk: 0)
    in_specs = [pl.BlockSpec((tm, tk), lambda i, j, *kk: (i, kidx(kk))),
                pl.BlockSpec((tk, tn), lambda i, j, *kk: (kidx(kk), j))]
    in_specs += [pl.BlockSpec((1, tn), lambda i, j, *kk: (0, j)) for _ in rows]
    in_specs += [pl.BlockSpec((tm, tn), lambda i, j, *kk: (i, j)) for _ in tiles]
    grid = (m // tm, n // tn) + ((nk,) if nk > 1 else ())
    sem = ("parallel", "parallel") + (("arbitrary",) if nk > 1 else ())
    return pl.pallas_call(
        functools.partial(_mm_kernel, nk=nk, n_extra=len(rows) + len(tiles), epilogue=epilogue),
        grid=grid,
        in_specs=in_specs,
        out_specs=pl.BlockSpec((tm, tn), lambda i, j, *kk: (i, j)),
        out_shape=jax.ShapeDtypeStruct((m, n), out_dtype),
        scratch_shapes=[pltpu.VMEM((tm, tn), F32)] if nk > 1 else [],
        compiler_params=_params(sem),
        name=name,
    )(a, b, *rows, *tiles)


def _ep_silu(acc):
    return _silu(acc)


def _ep_gelu(acc):
    return 0.5 * acc * (1.0 + lax.erf(acc * (1.0 / math.sqrt(2.0))))


def _ep_gate(acc, bias):
    return jax.nn.sigmoid(acc + bias)


def _ep_residual(acc, res):
    return acc + res


def _halo_specs(tl, tc, col_block, n_halo_blocks):
    r = tl // SUBLANES_BF16
    return [
        pl.BlockSpec((None, tl, tc), lambda b, i, j: (b, i, col_block + j)),
        pl.BlockSpec((None, SUBLANES_BF16, tc), lambda b, i, j: (b, jnp.maximum(i * r - 1, 0), col_block + j)),
        pl.BlockSpec((None, SUBLANES_BF16, tc),
                     lambda b, i, j: (b, jnp.minimum((i + 1) * r, n_halo_blocks - 1), col_block + j)),
    ]


def _dwconv_tile(main_ref, prev_ref, next_ref, w_ref, b_ref, ksize, n_row_blocks):
    i = pl.program_id(1)
    tl = main_ref.shape[0]
    has_prev = jnp.where(i > 0, 1.0, 0.0)
    has_next = jnp.where(i < n_row_blocks - 1, 1.0, 0.0)
    prev = prev_ref[...].astype(F32)[SUBLANES_BF16 - HALO_KEEP:] * has_prev
    nxt = next_ref[...].astype(F32)[:HALO_KEEP] * has_next
    ext = jnp.concatenate([prev, main_ref[...].astype(F32), nxt], axis=0)
    w = w_ref[...]
    pad = ksize // 2
    acc = b_ref[...]
    for j in range(ksize):
        off = HALO_KEEP - pad + j
        acc = acc + w[j:j + 1, :] * ext[off:off + tl]
    return acc


def _conv_kernel(*refs, n_groups, ksize, n_row_blocks, combine):
    outs = refs[5 * n_groups:]
    conv = []
    for g in range(n_groups):
        main_ref, prev_ref, next_ref, w_ref, b_ref = refs[5 * g:5 * g + 5]
        conv.append(_dwconv_tile(main_ref, prev_ref, next_ref, w_ref, b_ref, ksize, n_row_blocks))
    for o_ref, val in zip(outs, combine(*conv)):
        o_ref[...] = val.astype(o_ref.dtype)


def _dwconv_call(x, w, b, *, col_offsets, width, combine, n_out, tl, tc, name):
    bt, seq, _ = x.shape
    ksize = w.shape[0]
    tl, tc = _tile(seq, tl), _tile(width, tc)
    n_row_blocks = seq // tl
    w = w.astype(F32)
    b = b.reshape(1, -1).astype(F32)
    in_specs, args = [], []
    for off in col_offsets:
        cb = off // tc
        in_specs += _halo_specs(tl, tc, cb, seq // SUBLANES_BF16)
        in_specs += [pl.BlockSpec((ksize, tc), lambda bb, i, j, cb=cb: (0, cb + j)),
                     pl.BlockSpec((1, tc), lambda bb, i, j, cb=cb: (0, cb + j))]
        args += [x, x, x, w, b]
    out_spec = pl.BlockSpec((None, tl, tc), lambda bb, i, j: (bb, i, j))
    res = pl.pallas_call(
        functools.partial(_conv_kernel, n_groups=len(col_offsets), ksize=ksize, n_row_blocks=n_row_blocks,
                          combine=combine),
        grid=(bt, n_row_blocks, width // tc),
        in_specs=in_specs,
        out_specs=[out_spec] * n_out,
        out_shape=[jax.ShapeDtypeStruct((bt, seq, width), BF16)] * n_out,
        compiler_params=_params(("parallel", "parallel", "parallel")),
        name=name,
    )(*args)
    return res


def _split3(x):
    hi = x.astype(BF16)
    r1 = x - hi.astype(F32)
    mid = r1.astype(BF16)
    lo = (r1 - mid.astype(F32)).astype(BF16)
    return hi, mid, lo


def _ssd_kernel(*refs, reverse, finalize):
    if finalize:
        (xs_ref, b_ref, c_ref, dtr_ref, dtb_ref, alog_ref, e_ref,
         yf_ref, z_ref, dexp_ref, ng_ref, o_ref, h_ref) = refs
    else:
        xs_ref, b_ref, c_ref, dtr_ref, dtb_ref, alog_ref, e_ref, o_ref, h_ref = refs
    t = SSD_CHUNK
    head0 = SSM_HEADS if reverse else 0

    @pl.when(pl.program_id(1) == 0)
    def _():
        h_ref[...] = jnp.zeros_like(h_ref)

    dt = jax.nn.softplus(dtr_ref[...] + dtb_ref[...])
    da = dt * (-jnp.exp(alog_ref[...]))
    row = lax.broadcasted_iota(jnp.int32, (t, t), 0)
    col = lax.broadcasted_iota(jnp.int32, (t, t), 1)
    if reverse:
        tri = (row > col).astype(BF16)
        tri_t = (col > row).astype(BF16)
        mask = col >= row
    else:
        tri = (row >= col).astype(BF16)
        tri_t = (col >= row).astype(BF16)
        mask = row >= col
    pieces = _split3(da)
    p = sum(jnp.dot(tri, x, preferred_element_type=F32) for x in pieces)
    p_t = sum(lax.dot_general(x, tri_t, (((0,), (0,)), ((), ())), preferred_element_type=F32)
              for x in pieces)
    total = jnp.sum(da, axis=0, keepdims=True)
    if reverse:
        dd = dt * jnp.exp(p)
        ea = jnp.exp(total - p)
    else:
        dd = dt * jnp.exp(total - p)
        ea = jnp.exp(p)
    q = jnp.concatenate([dt, dd, ea], axis=0).astype(BF16)
    etot = _split3(jnp.broadcast_to(jnp.exp(total), (SUBLANES_BF16, 2 * SSM_HEADS)))
    lane = lax.broadcasted_iota(jnp.int32, (t, 4 * SSM_HEAD_DIM), 1)

    for g in range(SSM_GROUPS):
        cols = slice(g * SSM_GROUP_WIDTH, (g + 1) * SSM_GROUP_WIDTH)
        scols = slice(g * SSM_STATE, (g + 1) * SSM_STATE)
        eg = e_ref[:, cols]
        ex = jnp.dot(q, eg, preferred_element_type=F32)
        dt_e, dd_e, ea_e = ex[0:t], ex[t:2 * t], ex[2 * t:3 * t]
        etot_e = sum(jnp.dot(x, eg, preferred_element_type=F32) for x in etot)[0:1]
        xs = xs_ref[:, cols].astype(F32)
        xc = (xs * dt_e).astype(BF16)
        xc2 = (xs * dd_e).astype(BF16)
        bg = b_ref[:, scols]
        cg = c_ref[:, scols]
        cb = lax.dot_general(cg, bg, (((1,), (1,)), ((), ())), preferred_element_type=F32)
        h_in = h_ref[:, cols]
        y = jnp.dot(cg, h_in.astype(BF16), preferred_element_type=F32) * ea_e
        s_new = lax.dot_general(bg, xc2, (((0,), (0,)), ((), ())), preferred_element_type=F32)
        h_ref[:, cols] = h_in * etot_e + s_new
        quads = []
        for qd in range(2):
            xq = xc[:, qd * 4 * SSM_HEAD_DIM:(qd + 1) * 4 * SSM_HEAD_DIM]
            yq = None
            for j in range(4):
                hc = head0 + g * 8 + qd * 4 + j
                pc = jnp.broadcast_to(p[:, hc:hc + 1], (t, t))
                pr = jnp.broadcast_to(p_t[hc:hc + 1, :], (t, t))
                seg = (pr - pc) if reverse else (pc - pr)
                dec = jnp.exp(jnp.where(mask, seg, -jnp.inf))
                m = (cb * dec).astype(BF16)
                in_head = (lane >= j * SSM_HEAD_DIM) & (lane < (j + 1) * SSM_HEAD_DIM)
                xm = jnp.where(in_head, xq, jnp.zeros_like(xq))
                d = jnp.dot(m, xm, preferred_element_type=F32)
                yq = d if yq is None else yq + d
            quads.append(yq)
        y = y + jnp.concatenate(quads, axis=1)
        if finalize:
            y = yf_ref[:, cols] + y + xs * dexp_ref[:, cols]
            y = y * z_ref[:, cols].astype(F32)
            ms = jnp.mean(y * y, axis=-1, keepdims=True)
            y = y * lax.rsqrt(ms + RMS_EPS) * ng_ref[:, cols]
        o_ref[:, cols] = y.astype(o_ref.dtype)


def _ssd_call(xbc_act, dt_raw, dt_bias, a_log, e_map, *, reverse, y_fwd=None, z_act=None, d_exp=None,
              norm_g=None):
    bt, seq, _ = xbc_act.shape
    t = SSD_CHUNK
    nc = seq // t
    finalize = y_fwd is not None
    cidx = (lambda c: nc - 1 - c) if reverse else (lambda c: c)
    gn_blocks = SSM_WIDTH // SSM_GN
    in_specs = [
        pl.BlockSpec((None, t, SSM_WIDTH), lambda b, c: (b, cidx(c), 0)),
        pl.BlockSpec((None, t, SSM_GN), lambda b, c: (b, cidx(c), gn_blocks)),
        pl.BlockSpec((None, t, SSM_GN), lambda b, c: (b, cidx(c), gn_blocks + 1)),
        pl.BlockSpec((None, t, 2 * SSM_HEADS), lambda b, c: (b, cidx(c), 0)),
        pl.BlockSpec((1, 2 * SSM_HEADS), lambda b, c: (0, 0)),
        pl.BlockSpec((1, 2 * SSM_HEADS), lambda b, c: (0, 0)),
        pl.BlockSpec((2 * SSM_HEADS, SSM_WIDTH), lambda b, c: (0, 0)),
    ]
    args = [xbc_act, xbc_act, xbc_act, dt_raw, dt_bias, a_log, e_map]
    if finalize:
        in_specs += [
            pl.BlockSpec((None, t, SSM_WIDTH), lambda b, c: (b, cidx(c), 0)),
            pl.BlockSpec((None, t, SSM_WIDTH), lambda b, c: (b, cidx(c), 0)),
            pl.BlockSpec((1, SSM_WIDTH), lambda b, c: (0, 0)),
            pl.BlockSpec((1, SSM_WIDTH), lambda b, c: (0, 0)),
        ]
        args += [y_fwd, z_act, d_exp, norm_g]
    return pl.pallas_call(
        functools.partial(_ssd_kernel, reverse=reverse, finalize=finalize),
        grid=(bt, nc),
        in_specs=in_specs,
        out_specs=pl.BlockSpec((None, t, SSM_WIDTH), lambda b, c: (b, cidx(c), 0)),
        out_shape=jax.ShapeDtypeStruct((bt, seq, SSM_WIDTH), BF16 if finalize else F32),
        scratch_shapes=[pltpu.VMEM((SSM_STATE, SSM_WIDTH), F32)],
        compiler_params=_params(("parallel", "arbitrary")),
        name="ssd_bwd" if reverse else "ssd_fwd",
    )(*args)


def _hy_filter_kernel(fr_ref, w1t_ref, w1c_ref, w1s_ref, b1_ref, w2_ref, b2_ref, w3_ref, b3_ref, fq_ref,
                      w4_ref, dl_ref, ks_ref, kd_ref, *, n):
    tl = ks_ref.shape[0]
    pos = (pl.program_id(0) * tl + lax.broadcasted_iota(jnp.int32, (tl, 1), 0)).astype(F32)
    tt = pos / float(n - 1)
    arg = fr_ref[...] * ((2.0 * math.pi / n) * pos)
    fq = fq_ref[...]
    dot = functools.partial(jnp.dot, preferred_element_type=F32)
    h = tt * w1t_ref[...] + dot(jnp.cos(arg), w1c_ref[...]) + dot(-jnp.sin(arg), w1s_ref[...]) + b1_ref[...]
    h = jnp.sin(fq * h)
    h = jnp.sin(fq * (dot(h, w2_ref[...]) + b2_ref[...]))
    h = jnp.sin(fq * (dot(h, w3_ref[...]) + b3_ref[...]))
    k = dot(h, w4_ref[...])
    win = jnp.exp(-tt * dl_ref[...])
    kf = k[:, :HY_WIDTH] * win
    kb = jnp.where(pos == 0.0, 0.0, k[:, HY_WIDTH:] * win)
    ks_ref[...] = (kf + kb).astype(ks_ref.dtype)
    kd_ref[...] = (kf - kb).astype(kd_ref.dtype)


def _hy_filters(n, w1, b1, w2, b2, w3, b3, freq, w4):
    tl = _tile(n, 256)
    fr = jnp.linspace(1e-4, HY_BANDS - 1, HY_BANDS, dtype=F32).reshape(1, HY_BANDS)
    min_decay = math.log(HY_TARGET) / HY_SLOW_DECAY
    max_decay = math.log(HY_TARGET) / HY_FAST_DECAY
    deltas = jnp.abs(jnp.linspace(min_decay, max_decay, HY_WIDTH, dtype=F32)).reshape(1, HY_WIDTH)
    w1 = w1.astype(F32)
    small = [fr, w1[0:1], w1[1:1 + HY_BANDS], w1[1 + HY_BANDS:], b1.reshape(1, -1), w2, b2.reshape(1, -1),
             w3, b3.reshape(1, -1), freq.reshape(1, -1), w4, deltas]
    small = [s.astype(F32) for s in small]
    out_spec = pl.BlockSpec((tl, HY_WIDTH), lambda i: (i, 0))
    return pl.pallas_call(
        functools.partial(_hy_filter_kernel, n=n),
        grid=(n // tl,),
        in_specs=[pl.BlockSpec(s.shape, lambda i: (0, 0)) for s in small],
        out_specs=[out_spec, out_spec],
        out_shape=[jax.ShapeDtypeStruct((n, HY_WIDTH), BF16)] * 2,
        compiler_params=_params(("parallel",)),
        name="hy_filter",
    )(*small)


def _dft_gen_kernel(c_ref, s_ref, *, n, inverse):
    tm, tn = c_ref.shape
    r = pl.program_id(0) * tm + lax.broadcasted_iota(jnp.int32, (tm, tn), 0)
    c = pl.program_id(1) * tn + lax.broadcasted_iota(jnp.int32, (tm, tn), 1)
    m = ((2 * c + 1) * r) if inverse else ((2 * r + 1) * c)
    m = jnp.bitwise_and(m, 4 * n - 1)
    m = jnp.where(m >= 2 * n, m - 4 * n, m)
    ang = m.astype(F32) * (math.pi / (2 * n))
    scale = (1.0 / n) if inverse else 1.0
    c_ref[...] = (jnp.cos(ang) * scale).astype(c_ref.dtype)
    s_ref[...] = (jnp.sin(ang) * (-scale)).astype(s_ref.dtype)


def _dft_matrices(n, inverse):
    tm, tn = _tile(n, 256), _tile(n, 512)
    spec = pl.BlockSpec((tm, tn), lambda i, j: (i, j))
    return pl.pallas_call(
        functools.partial(_dft_gen_kernel, n=n, inverse=inverse),
        grid=(n // tm, n // tn),
        out_specs=[spec, spec],
        out_shape=[jax.ShapeDtypeStruct((n, n), BF16)] * 2,
        compiler_params=_params(("parallel", "parallel")),
        name="dft_gen_inv" if inverse else "dft_gen_fwd",
    )()


def _hy_fwd_kernel(co_ref, sn_ref, w_ref, kre_ref, kim_ref, yre_ref, yim_ref):
    w = w_ref[...]
    wre = jnp.dot(co_ref[...], w, preferred_element_type=F32)
    wim = jnp.dot(sn_ref[...], w, preferred_element_type=F32)
    kre, kim = kre_ref[...], kim_ref[...]
    yre_ref[...] = (wre * kre - wim * kim).astype(yre_ref.dtype)
    yim_ref[...] = (wre * kim + wim * kre).astype(yim_ref.dtype)


def _hy_fwd_call(co, sn, w, kre, kim):
    bt, n, width = w.shape
    tm, tn = _tile(n, 512), _tile(width, 512)
    fspec = pl.BlockSpec((tm, n), lambda i, b, j: (i, 0))
    kspec = pl.BlockSpec((tm, tn), lambda i, b, j: (i, j))
    ospec = pl.BlockSpec((None, tm, tn), lambda i, b, j: (b, i, j))
    return pl.pallas_call(
        _hy_fwd_kernel,
        grid=(n // tm, bt, width // tn),
        in_specs=[fspec, fspec, pl.BlockSpec((None, n, tn), lambda i, b, j: (b, 0, j)), kspec, kspec],
        out_specs=[ospec, ospec],
        out_shape=[jax.ShapeDtypeStruct((bt, n, width), BF16)] * 2,
        compiler_params=_params(("parallel", "parallel", "parallel")),
        name="hy_dft_fwd",
    )(co, sn, w, kre, kim)


def _hy_inv_kernel(ct_ref, st_ref, yre_ref, yim_ref, x0_ref, w_ref, bias_ref, o_ref):
    yc = (jnp.dot(ct_ref[...], yre_ref[...], preferred_element_type=F32)
          + jnp.dot(st_ref[...], yim_ref[...], preferred_element_type=F32))
    o_ref[...] = (x0_ref[...].astype(F32) * (yc + w_ref[...].astype(F32) * bias_ref[...])).astype(o_ref.dtype)


def _hy_inv_call(ct, st, yre, yim, x0, w, bias):
    bt, n, width = w.shape
    tm, tn = _tile(n, 512), _tile(width, 512)
    fspec = pl.BlockSpec((tm, n), lambda i, b, j: (i, 0))
    yspec = pl.BlockSpec((None, n, tn), lambda i, b, j: (b, 0, j))
    tspec = pl.BlockSpec((None, tm, tn), lambda i, b, j: (b, i, j))
    return pl.pallas_call(
        _hy_inv_kernel,
        grid=(n // tm, bt, width // tn),
        in_specs=[fspec, fspec, yspec, yspec, tspec, tspec, pl.BlockSpec((1, tn), lambda i, b, j: (0, j))],
        out_specs=tspec,
        out_shape=jax.ShapeDtypeStruct((bt, n, width), BF16),
        compiler_params=_params(("parallel", "parallel", "parallel")),
        name="hy_dft_inv",
    )(ct, st, yre, yim, x0, w, bias.reshape(1, width).astype(F32))


def _sgu_kernel(x_ref, lng_ref, lnb_ref, ws_ref, bse_ref, o_ref):
    tl = x_ref.shape[0]
    v = x_ref[:, SG_WIDTH:].astype(F32)
    mu = jnp.mean(v, axis=-1, keepdims=True)
    vc = v - mu
    var = jnp.mean(vc * vc, axis=-1, keepdims=True)
    vn = (vc * lax.rsqrt(var + LN_EPS) * lng_ref[...] + lnb_ref[...]).astype(BF16)
    for g in range(SG_GROUPS):
        cols = slice(g * SG_GROUP_DIM, (g + 1) * SG_GROUP_DIM)
        wsg = ws_ref[g]
        for k in range(tl // SG_CHUNK):
            rows = slice(k * SG_CHUNK, (k + 1) * SG_CHUNK)
            mixed = jnp.dot(wsg, vn[rows, cols], preferred_element_type=F32) + bse_ref[:, cols]
            o_ref[rows, cols] = (x_ref[rows, cols].astype(F32) * mixed).astype(o_ref.dtype)


def _sgu_call(sg_act, ln_g, ln_b, ws, bs):
    bt, seq, _ = sg_act.shape
    tl = _tile(seq, 2 * SG_CHUNK)
    bs_exp = jnp.repeat(bs.astype(F32).T, SG_GROUP_DIM, axis=1)
    return pl.pallas_call(
        _sgu_kernel,
        grid=(bt, seq // tl),
        in_specs=[pl.BlockSpec((None, tl, SG_IN), lambda b, i: (b, i, 0)),
                  pl.BlockSpec((1, SG_WIDTH), lambda b, i: (0, 0)),
                  pl.BlockSpec((1, SG_WIDTH), lambda b, i: (0, 0)),
                  pl.BlockSpec((SG_GROUPS, SG_CHUNK, SG_CHUNK), lambda b, i: (0, 0, 0)),
                  pl.BlockSpec((SG_CHUNK, SG_WIDTH), lambda b, i: (0, 0))],
        out_specs=pl.BlockSpec((None, tl, SG_WIDTH), lambda b, i: (b, i, 0)),
        out_shape=jax.ShapeDtypeStruct((bt, seq, SG_WIDTH), BF16),
        compiler_params=_params(("parallel", "parallel")),
        name="sgu",
    )(sg_act, ln_g.reshape(1, -1).astype(F32), ln_b.reshape(1, -1).astype(F32), ws.astype(BF16), bs_exp)


def _merge_kernel(ym_ref, yh_ref, yg_ref, w0_ref, w1_ref, w2_ref, g0_ref, g1_ref, g2_ref, o_ref):
    dot = functools.partial(jnp.dot, preferred_element_type=F32)
    acc = g0_ref[...].astype(F32) * dot(ym_ref[...], w0_ref[...])
    acc = acc + g1_ref[...].astype(F32) * dot(yh_ref[...], w1_ref[...])
    acc = acc + g2_ref[...].astype(F32) * dot(yg_ref[...], w2_ref[...])
    o_ref[...] = acc.astype(o_ref.dtype)


def _merge_call(y_m, y_h, y_g, w0, w1, w2, gates):
    m = y_m.shape[0]
    tm, tn = _tile(m, 512), 512
    nb = D_MODEL // tn
    aspec = lambda width: pl.BlockSpec((tm, width), lambda i, j: (i, 0))
    wspec = lambda width: pl.BlockSpec((width, tn), lambda i, j: (0, j))
    gspec = lambda br: pl.BlockSpec((tm, tn), lambda i, j: (i, br * nb + j))
    return pl.pallas_call(
        _merge_kernel,
        grid=(m // tm, nb),
        in_specs=[aspec(SSM_WIDTH), aspec(HY_WIDTH), aspec(SG_WIDTH), wspec(SSM_WIDTH), wspec(HY_WIDTH),
                  wspec(SG_WIDTH), gspec(0), gspec(1), gspec(2)],
        out_specs=pl.BlockSpec((tm, tn), lambda i, j: (i, j)),
        out_shape=jax.ShapeDtypeStruct((m, D_MODEL), BF16),
        compiler_params=_params(("parallel", "parallel")),
        name="merge",
    )(y_m, y_h, y_g, w0, w1, w2, gates, gates, gates)


def _prep_layer(l, p):
    w_in = p["w_in"][l]
    o_xbc, o_dt, o_hy = SSM_WIDTH, SSM_WIDTH + SSM_XBC, SSM_IN
    o_sg, o_gate = SSM_IN + HY_IN, SSM_IN + HY_IN + SG_IN
    cut = lambda a, b: w_in[:, a:b].astype(BF16)
    w_br = p["w_br"][l]
    heads = jnp.arange(2 * SSM_HEADS)[:, None]
    chan_head = (jnp.arange(SSM_WIDTH) // SSM_HEAD_DIM)[None, :]
    return dict(
        norm1_g=p["norm1_g"][l], norm2_g=p["norm2_g"][l],
        w_z=cut(0, o_xbc), w_xbc=cut(o_xbc, o_dt), w_dt=cut(o_dt, o_hy), w_hy=cut(o_hy, o_sg),
        w_sg=cut(o_sg, o_gate), w_gate=cut(o_gate, o_gate + GATE_IN),
        b_gate=p["b_gate"][l].reshape(1, -1).astype(F32),
        ssm_conv_w=p["ssm_conv_w"][l], ssm_conv_b=p["ssm_conv_b"][l],
        dt_bias=p["ssm_dt_bias"][l].reshape(1, -1).astype(F32),
        a_log=p["ssm_a_log"][l].reshape(1, -1).astype(F32),
        e_fwd=(heads == chan_head).astype(BF16), e_bwd=(heads == chan_head + SSM_HEADS).astype(BF16),
        d_exp=jnp.repeat(p["ssm_d"][l].astype(F32), SSM_HEAD_DIM).reshape(1, -1),
        ssm_norm_g=p["ssm_norm_g"][l].reshape(1, -1).astype(F32),
        hy_conv_w=p["hy_conv_w"][l], hy_conv_b=p["hy_conv_b"][l],
        hy_mlp=tuple(p[k][l] for k in ("hy_w1", "hy_b1", "hy_w2", "hy_b2", "hy_w3", "hy_b3", "hy_freq", "hy_w4")),
        hy_bias=p["hy_bias"][l],
        sg_ln_g=p["sg_ln_g"][l], sg_ln_b=p["sg_ln_b"][l], sg_ws=p["sg_ws"][l], sg_bs=p["sg_bs"][l],
        w_br0=w_br[:SSM_WIDTH].astype(BF16), w_br1=w_br[SSM_WIDTH:SSM_WIDTH + HY_WIDTH].astype(BF16),
        w_br2=w_br[SSM_WIDTH + HY_WIDTH:].astype(BF16),
        w_out=p["w_out"][l].astype(BF16), w_up=p["w_up"][l].astype(BF16),
        ffn_conv_w=p["ffn_conv_w"][l], ffn_conv_b=p["ffn_conv_b"][l], w_down=p["w_down"][l].astype(BF16),
    )


def _ssd_branch(lw, xbc, dt_raw, z_act, bt, seq):
    xbc_act = _dwconv_call(xbc.reshape(bt, seq, SSM_XBC), lw["ssm_conv_w"], lw["ssm_conv_b"], col_offsets=(0,),
                           width=SSM_XBC, combine=lambda c: (_silu(c),), n_out=1, tl=512, tc=512,
                           name="ssm_conv")[0]
    dt3 = dt_raw.reshape(bt, seq, 2 * SSM_HEADS)
    y_fwd = _ssd_call(xbc_act, dt3, lw["dt_bias"], lw["a_log"], lw["e_fwd"], reverse=False)
    y_m = _ssd_call(xbc_act, dt3, lw["dt_bias"], lw["a_log"], lw["e_bwd"], reverse=True, y_fwd=y_fwd,
                    z_act=z_act.reshape(bt, seq, SSM_WIDTH), d_exp=lw["d_exp"], norm_g=lw["ssm_norm_g"])
    return y_m.reshape(bt * seq, SSM_WIDTH)


def _hyena_branch(lw, hy, dft, bt, seq):
    co, sn, ct, st = dft
    x0, w = _dwconv_call(hy.reshape(bt, seq, HY_IN), lw["hy_conv_w"], lw["hy_conv_b"],
                         col_offsets=(0, HY_WIDTH, 2 * HY_WIDTH), width=HY_WIDTH,
                         combine=lambda c0, c1, c2: (c0, c2 * c1), n_out=2, tl=512, tc=512, name="hy_conv")
    ks, kd = _hy_filters(seq, *lw["hy_mlp"])
    kre = _matmul(co, ks, out_dtype=F32, name="hy_kre", tm=512, tn=512)
    kim = _matmul(sn, kd, out_dtype=F32, name="hy_kim", tm=512, tn=512)
    yre, yim = _hy_fwd_call(co, sn, w, kre, kim)
    y_h = _hy_inv_call(ct, st, yre, yim, x0, w, lw["hy_bias"])
    return y_h.reshape(bt * seq, HY_WIDTH)


def _layer(x, lw, dft, bt, seq):
    h = _rmsnorm(x, lw["norm1_g"], BF16)
    z_act = _matmul(h, lw["w_z"], out_dtype=BF16, name="in_z", epilogue=_ep_silu)
    xbc = _matmul(h, lw["w_xbc"], out_dtype=BF16, name="in_xbc")
    dt_raw = _matmul(h, lw["w_dt"], out_dtype=F32, name="in_dt")
    hy = _matmul(h, lw["w_hy"], out_dtype=BF16, name="in_hy")
    sg_act = _matmul(h, lw["w_sg"], out_dtype=BF16, name="in_sg", epilogue=_ep_gelu)
    gates = _matmul(h, lw["w_gate"], out_dtype=BF16, name="in_gate", epilogue=_ep_gate, rows=(lw["b_gate"],))
    y_m = _ssd_branch(lw, xbc, dt_raw, z_act, bt, seq)
    y_h = _hyena_branch(lw, hy, dft, bt, seq)
    y_g = _sgu_call(sg_act.reshape(bt, seq, SG_IN), lw["sg_ln_g"], lw["sg_ln_b"], lw["sg_ws"],
                    lw["sg_bs"]).reshape(bt * seq, SG_WIDTH)
    merged = _merge_call(y_m, y_h, y_g, lw["w_br0"], lw["w_br1"], lw["w_br2"], gates)
    x = _matmul(merged, lw["w_out"], out_dtype=F32, name="out_proj", epilogue=_ep_residual, tiles=(x,))
    h2 = _rmsnorm(x, lw["norm2_g"], BF16)
    up = _matmul(h2, lw["w_up"], out_dtype=BF16, name="ffn_up", tn=512)
    act = _dwconv_call(up.reshape(bt, seq, 2 * D_FF), lw["ffn_conv_w"], lw["ffn_conv_b"], col_offsets=(0, D_FF),
                       width=D_FF, combine=lambda g, v: (_silu(g) * v,), n_out=1, tl=512, tc=256,
                       name="ffn_conv")[0]
    return _matmul(act.reshape(bt * seq, D_FF), lw["w_down"], out_dtype=F32, name="ffn_down",
                   epilogue=_ep_residual, tiles=(x,), tn=512, nk=2)


def _trunk(x, layers, normf_g):
    bt, seq, d = x.shape
    dft = (*_dft_matrices(seq, inverse=False), *_dft_matrices(seq, inverse=True))
    xf = x.reshape(bt * seq, d)
    for lw in layers:
        xf = _layer(xf, lw, dft, bt, seq)
    return _rmsnorm(xf, normf_g, F32).reshape(bt, seq, d)


def kernel(x_prompt, x_sample, norm1_g, w_in, b_gate, ssm_conv_w, ssm_conv_b, ssm_dt_bias, ssm_a_log, ssm_d,
           ssm_norm_g, hy_conv_w, hy_conv_b, hy_w1, hy_b1, hy_w2, hy_b2, hy_w3, hy_b3, hy_freq, hy_w4, hy_bias,
           sg_ln_g, sg_ln_b, sg_ws, sg_bs, w_br, w_out, norm2_g, w_up, ffn_conv_w, ffn_conv_b, w_down, normf_g):
    p = dict(norm1_g=norm1_g, w_in=w_in, b_gate=b_gate, ssm_conv_w=ssm_conv_w, ssm_conv_b=ssm_conv_b,
             ssm_dt_bias=ssm_dt_bias, ssm_a_log=ssm_a_log, ssm_d=ssm_d, ssm_norm_g=ssm_norm_g,
             hy_conv_w=hy_conv_w, hy_conv_b=hy_conv_b, hy_w1=hy_w1, hy_b1=hy_b1, hy_w2=hy_w2, hy_b2=hy_b2,
             hy_w3=hy_w3, hy_b3=hy_b3, hy_freq=hy_freq, hy_w4=hy_w4, hy_bias=hy_bias, sg_ln_g=sg_ln_g,
             sg_ln_b=sg_ln_b, sg_ws=sg_ws, sg_bs=sg_bs, w_br=w_br, w_out=w_out, norm2_g=norm2_g, w_up=w_up,
             ffn_conv_w=ffn_conv_w, ffn_conv_b=ffn_conv_b, w_down=w_down)
    layers = [_prep_layer(l, p) for l in range(w_in.shape[0])]
    return (_trunk(x_prompt, layers, normf_g), _trunk(x_sample, layers, normf_g))
```

```python
import functools
import math

import jax
import jax.numpy as jnp
from jax import lax
from jax.experimental import pallas as pl
from jax.experimental.pallas import tpu as pltpu

F32 = jnp.float32
BF16 = jnp.bfloat16

D_MODEL = 4096
SSM_WIDTH = D_MODEL
SSM_HEAD_DIM = 64
SSM_HEADS = SSM_WIDTH // SSM_HEAD_DIM
SSM_GROUPS = 8
SSM_STATE = 128
SSM_CONV = 5
SSD_CHUNK = 128
SSM_GN = SSM_GROUPS * SSM_STATE
SSM_XBC = SSM_WIDTH + 2 * SSM_GN
SSM_IN = SSM_WIDTH + SSM_XBC + 2 * SSM_HEADS
SSM_GROUP_WIDTH = SSM_WIDTH // SSM_GROUPS
HY_WIDTH = D_MODEL // 2
HY_SHORT = 3
HY_EMB = 33
HY_BANDS = (HY_EMB - 1) // 2
HY_ORDER = 64
HY_FAST_DECAY = 0.3
HY_SLOW_DECAY = 1.5
HY_TARGET = 1e-2
HY_IN = 3 * HY_WIDTH
SG_WIDTH = D_MODEL // 2
SG_CHUNK = 128
SG_GROUPS = 16
SG_GROUP_DIM = SG_WIDTH // SG_GROUPS
SG_IN = 2 * SG_WIDTH
N_BRANCH = 3
GATE_IN = N_BRANCH * D_MODEL
D_FF = 256 * math.ceil(8 * D_MODEL / 3 / 256)
FFN_CONV = 3
RMS_EPS = 1e-6
LN_EPS = 1e-5

V7X_SCOPED_VMEM_BYTES = 56 * 1024 * 1024
LANES = 128
SUBLANES_BF16 = 16
HALO_KEEP = 8
MM_CONV_PIECE_ROWS = 128
MM_PIECE_ROWS = 256


def _tile(dim, pref):
    t = pref
    while dim % t:
        t //= 2
    return t


def _params(semantics):
    return pltpu.CompilerParams(dimension_semantics=semantics, vmem_limit_bytes=V7X_SCOPED_VMEM_BYTES)


def _silu(x):
    return x * jax.nn.sigmoid(x)


def _rmsnorm_kernel(x_ref, g_ref, o_ref):
    x = x_ref[...]
    ms = jnp.mean(x * x, axis=-1, keepdims=True)
    o_ref[...] = (x * lax.rsqrt(ms + RMS_EPS) * g_ref[...]).astype(o_ref.dtype)


def _rmsnorm(x, g, out_dtype):
    m, d = x.shape
    tr = _tile(m, 256)
    return pl.pallas_call(
        _rmsnorm_kernel,
        grid=(m // tr,),
        in_specs=[pl.BlockSpec((tr, d), lambda i: (i, 0)), pl.BlockSpec((1, d), lambda i: (0, 0))],
        out_specs=pl.BlockSpec((tr, d), lambda i: (i, 0)),
        out_shape=jax.ShapeDtypeStruct((m, d), out_dtype),
        compiler_params=_params(("parallel",)),
        name="rmsnorm",
    )(x, g.reshape(1, d).astype(F32))


def _mm_kernel(a_ref, b_ref, *rest, nk, n_extra, epilogue):
    extra = rest[:n_extra]
    o_ref = rest[n_extra]

    def finish(acc):
        if epilogue is not None:
            acc = epilogue(acc, *[r[...] for r in extra])
        o_ref[...] = acc.astype(o_ref.dtype)

    if nk == 1:
        tm = a_ref.shape[0]
        pm = min(tm, MM_PIECE_ROWS)
        for p in range(tm // pm):
            rows = slice(p * pm, (p + 1) * pm)
            acc = jnp.dot(a_ref[rows, :], b_ref[...], preferred_element_type=F32)
            if epilogue is not None:
                acc = epilogue(acc, *[r[...] if r.shape[0] == 1 else r[rows, :] for r in extra])
            o_ref[rows, :] = acc.astype(o_ref.dtype)
    else:
        acc_ref = rest[n_extra + 1]
        k = pl.program_id(2)

        @pl.when(k == 0)
        def _():
            acc_ref[...] = jnp.zeros_like(acc_ref)

        acc_ref[...] += jnp.dot(a_ref[...], b_ref[...], preferred_element_type=F32)

        @pl.when(k == nk - 1)
        def _():
            finish(acc_ref[...])


def _matmul(a, b, *, out_dtype, name, epilogue=None, rows=(), tiles=(), tm=1024, tn=1024, nk=1):
    m, k = a.shape
    _, n = b.shape
    tm, tn, tk = _tile(m, tm), _tile(n, tn), k // nk
    kidx = (lambda kk: kk[0]) if nk > 1 else (lambda kk: 0)
    in_specs = [pl.BlockSpec((tm, tk), lambda i, j, *kk: (i, kidx(kk))),
                pl.BlockSpec((tk, tn), lambda i, j, *kk: (kidx(kk), j))]
    in_specs += [pl.BlockSpec((1, tn), lambda i, j, *kk: (0, j)) for _ in rows]
    in_specs += [pl.BlockSpec((tm, tn), lambda i, j, *kk: (i, j)) for _ in tiles]
    grid = (m // tm, n // tn) + ((nk,) if nk > 1 else ())
    sem = ("parallel", "parallel") + (("arbitrary",) if nk > 1 else ())
    return pl.pallas_call(
        functools.partial(_mm_kernel, nk=nk, n_extra=len(rows) + len(tiles), epilogue=epilogue),
        grid=grid,
        in_specs=in_specs,
        out_specs=pl.BlockSpec((tm, tn), lambda i, j, *kk: (i, j)),
        out_shape=jax.ShapeDtypeStruct((m, n), out_dtype),
        scratch_shapes=[pltpu.VMEM((tm, tn), F32)] if nk > 1 else [],
        compiler_params=_params(sem),
        name=name,
    )(a, b, *rows, *tiles)


def _ep_silu(acc):
    return _silu(acc)


def _ep_gelu(acc):
    return 0.5 * acc * (1.0 + lax.erf(acc * (1.0 / math.sqrt(2.0))))


def _ep_gate(acc, bias):
    return jax.nn.sigmoid(acc + bias)


def _ep_residual(acc, res):
    return acc + res


def _mm_conv_kernel(a_ref, *refs, n_groups, ksize, n_row_tiles, tiles_per_seq, combine):
    w_refs = refs[:n_groups]
    cw_refs = refs[n_groups:2 * n_groups]
    cb_refs = refs[2 * n_groups:3 * n_groups]
    out_refs = refs[3 * n_groups:-2]
    p_ref, carry_ref = refs[-2:]
    s = pl.program_id(0)
    tm = a_ref.shape[0]
    pad = ksize // 2

    @pl.when(s == 0)
    def _():
        p_ref[...] = jnp.zeros_like(p_ref)
        carry_ref[...] = jnp.zeros_like(carry_ref)

    prev_tile = jnp.maximum(s - 1, 0) % n_row_tiles
    has_prev = jnp.where(prev_tile % tiles_per_seq != 0, 1.0, 0.0)
    has_next = jnp.where(prev_tile % tiles_per_seq != tiles_per_seq - 1, 1.0, 0.0)
    pm = min(tm, MM_CONV_PIECE_ROWS)
    n_pieces = tm // pm
    cur = [[None] * n_pieces for _ in range(n_groups)]
    for k in range(n_pieces):
        rows = slice(k * pm, (k + 1) * pm)
        a = a_ref[rows, :]
        for g in range(n_groups):
            cur[g][k] = jnp.dot(a, w_refs[g][...], preferred_element_type=F32)
        conv = []
        for g in range(n_groups):
            head = carry_ref[g] * has_prev if k == 0 else p_ref[g, k * pm - HALO_KEEP:k * pm, :]
            tail = (cur[g][0][:HALO_KEEP] * has_next if k == n_pieces - 1
                    else p_ref[g, (k + 1) * pm:(k + 1) * pm + HALO_KEEP, :])
            ext = jnp.concatenate([head, p_ref[g, rows, :], tail], axis=0)
            w = cw_refs[g][...]
            acc = cb_refs[g][...]
            for j in range(ksize):
                off = HALO_KEEP - pad + j
                acc = acc + w[j:j + 1, :] * ext[off:off + pm]
            conv.append(acc)
        for o_ref, val in zip(out_refs, combine(*conv)):
            o_ref[rows, :] = val.astype(o_ref.dtype)
    for g in range(n_groups):
        carry_ref[g] = p_ref[g, tm - HALO_KEEP:, :]
        for k in range(n_pieces):
            p_ref[g, k * pm:(k + 1) * pm, :] = cur[g][k]


def _mm_conv(a, w, cw, cb, *, seq, col_offsets, width, combine, n_out, tm, tn, name):
    t, k = a.shape
    ksize = cw.shape[0]
    tm, tn = _tile(seq, tm), _tile(width, tn)
    n_row_tiles, n_col_tiles = t // tm, width // tn
    steps = n_row_tiles * n_col_tiles
    n_groups = len(col_offsets)
    cw = cw.astype(F32)
    cb = cb.reshape(1, -1).astype(F32)
    cur_tile = lambda s: jnp.minimum(s, steps - 1)
    fin_tile = lambda s: jnp.maximum(s - 1, 0)
    in_specs = [pl.BlockSpec((tm, k), lambda s: (cur_tile(s) % n_row_tiles, 0))]
    in_specs += [pl.BlockSpec((k, tn), lambda s, o=off // tn: (0, o + cur_tile(s) // n_row_tiles))
                 for off in col_offsets]
    in_specs += [pl.BlockSpec((ksize, tn), lambda s, o=off // tn: (0, o + fin_tile(s) // n_row_tiles))
                 for off in col_offsets]
    in_specs += [pl.BlockSpec((1, tn), lambda s, o=off // tn: (0, o + fin_tile(s) // n_row_tiles))
                 for off in col_offsets]
    out_spec = pl.BlockSpec((tm, tn), lambda s: (fin_tile(s) % n_row_tiles, fin_tile(s) // n_row_tiles))
    return pl.pallas_call(
        functools.partial(_mm_conv_kernel, n_groups=n_groups, ksize=ksize, n_row_tiles=n_row_tiles,
                          tiles_per_seq=seq // tm, combine=combine),
        grid=(steps + 1,),
        in_specs=in_specs,
        out_specs=[out_spec] * n_out,
        out_shape=[jax.ShapeDtypeStruct((t, width), BF16)] * n_out,
        scratch_shapes=[pltpu.VMEM((n_groups, tm, tn), F32), pltpu.VMEM((n_groups, HALO_KEEP, tn), F32)],
        compiler_params=_params(("arbitrary",)),
        name=name,
    )(a, *([w] * n_groups), *([cw] * n_groups), *([cb] * n_groups))


def _split3(x):
    hi = x.astype(BF16)
    r1 = x - hi.astype(F32)
    mid = r1.astype(BF16)
    lo = (r1 - mid.astype(F32)).astype(BF16)
    return hi, mid, lo


def _ssd_kernel(*refs, reverse, finalize):
    if finalize:
        (xs_ref, b_ref, c_ref, dtr_ref, dtb_ref, alog_ref, e_ref,
         yf_ref, z_ref, dexp_ref, ng_ref, o_ref, h_ref) = refs
    else:
        xs_ref, b_ref, c_ref, dtr_ref, dtb_ref, alog_ref, e_ref, o_ref, h_ref = refs
    t = SSD_CHUNK
    head0 = SSM_HEADS if reverse else 0

    @pl.when(pl.program_id(1) == 0)
    def _():
        h_ref[...] = jnp.zeros_like(h_ref)

    dt = jax.nn.softplus(dtr_ref[...] + dtb_ref[...])
    da = dt * (-jnp.exp(alog_ref[...]))
    row = lax.broadcasted_iota(jnp.int32, (t, t), 0)
    col = lax.broadcasted_iota(jnp.int32, (t, t), 1)
    if reverse:
        tri = (row > col).astype(BF16)
        tri_t = (col > row).astype(BF16)
        mask = col >= row
    else:
        tri = (row >= col).astype(BF16)
        tri_t = (col >= row).astype(BF16)
        mask = row >= col
    pieces = _split3(da)
    p = sum(jnp.dot(tri, x, preferred_element_type=F32) for x in pieces)
    p_t = sum(lax.dot_general(x, tri_t, (((0,), (0,)), ((), ())), preferred_element_type=F32)
              for x in pieces)
    total = jnp.sum(da, axis=0, keepdims=True)
    if reverse:
        dd = dt * jnp.exp(p)
        ea = jnp.exp(total - p)
    else:
        dd = dt * jnp.exp(total - p)
        ea = jnp.exp(p)
    q = jnp.concatenate([dt, dd, ea], axis=0).astype(BF16)
    etot = _split3(jnp.broadcast_to(jnp.exp(total), (SUBLANES_BF16, 2 * SSM_HEADS)))
    lane = lax.broadcasted_iota(jnp.int32, (t, 4 * SSM_HEAD_DIM), 1)

    for g in range(SSM_GROUPS):
        cols = slice(g * SSM_GROUP_WIDTH, (g + 1) * SSM_GROUP_WIDTH)
        scols = slice(g * SSM_STATE, (g + 1) * SSM_STATE)
        eg = e_ref[:, cols]
        ex = jnp.dot(q, eg, preferred_element_type=F32)
        dt_e, dd_e, ea_e = ex[0:t], ex[t:2 * t], ex[2 * t:3 * t]
        etot_e = sum(jnp.dot(x, eg, preferred_element_type=F32) for x in etot)[0:1]
        xs = xs_ref[:, cols].astype(F32)
        xc = (xs * dt_e).astype(BF16)
        xc2 = (xs * dd_e).astype(BF16)
        bg = b_ref[:, scols]
        cg = c_ref[:, scols]
        cb = lax.dot_general(cg, bg, (((1,), (1,)), ((), ())), preferred_element_type=F32)
        h_in = h_ref[:, cols]
        y = jnp.dot(cg, h_in.astype(BF16), preferred_element_type=F32) * ea_e
        s_new = lax.dot_general(bg, xc2, (((0,), (0,)), ((), ())), preferred_element_type=F32)
        h_ref[:, cols] = h_in * etot_e + s_new
        quads = []
        for qd in range(2):
            xq = xc[:, qd * 4 * SSM_HEAD_DIM:(qd + 1) * 4 * SSM_HEAD_DIM]
            yq = None
            for j in range(4):
                hc = head0 + g * 8 + qd * 4 + j
                pc = jnp.broadcast_to(p[:, hc:hc + 1], (t, t))
                pr = jnp.broadcast_to(p_t[hc:hc + 1, :], (t, t))
                seg = (pr - pc) if reverse else (pc - pr)
                dec = jnp.exp(jnp.where(mask, seg, -jnp.inf))
                m = (cb * dec).astype(BF16)
                in_head = (lane >= j * SSM_HEAD_DIM) & (lane < (j + 1) * SSM_HEAD_DIM)
                xm = jnp.where(in_head, xq, jnp.zeros_like(xq))
                d = jnp.dot(m, xm, preferred_element_type=F32)
                yq = d if yq is None else yq + d
            quads.append(yq)
        y = y + jnp.concatenate(quads, axis=1)
        if finalize:
            y = yf_ref[:, cols] + y + xs * dexp_ref[:, cols]
            y = y * z_ref[:, cols].astype(F32)
            ms = jnp.mean(y * y, axis=-1, keepdims=True)
            y = y * lax.rsqrt(ms + RMS_EPS) * ng_ref[:, cols]
        o_ref[:, cols] = y.astype(o_ref.dtype)


def _ssd_call(xbc_act, dt_raw, dt_bias, a_log, e_map, *, reverse, y_fwd=None, z_act=None, d_exp=None,
              norm_g=None):
    bt, seq, _ = xbc_act.shape
    t = SSD_CHUNK
    nc = seq // t
    finalize = y_fwd is not None
    cidx = (lambda c: nc - 1 - c) if reverse else (lambda c: c)
    gn_blocks = SSM_WIDTH // SSM_GN
    in_specs = [
        pl.BlockSpec((None, t, SSM_WIDTH), lambda b, c: (b, cidx(c), 0)),
        pl.BlockSpec((None, t, SSM_GN), lambda b, c: (b, cidx(c), gn_blocks)),
        pl.BlockSpec((None, t, SSM_GN), lambda b, c: (b, cidx(c), gn_blocks + 1)),
        pl.BlockSpec((None, t, 2 * SSM_HEADS), lambda b, c: (b, cidx(c), 0)),
        pl.BlockSpec((1, 2 * SSM_HEADS), lambda b, c: (0, 0)),
        pl.BlockSpec((1, 2 * SSM_HEADS), lambda b, c: (0, 0)),
        pl.BlockSpec((2 * SSM_HEADS, SSM_WIDTH), lambda b, c: (0, 0)),
    ]
    args = [xbc_act, xbc_act, xbc_act, dt_raw, dt_bias, a_log, e_map]
    if finalize:
        in_specs += [
            pl.BlockSpec((None, t, SSM_WIDTH), lambda b, c: (b, cidx(c), 0)),
            pl.BlockSpec((None, t, SSM_WIDTH), lambda b, c: (b, cidx(c), 0)),
            pl.BlockSpec((1, SSM_WIDTH), lambda b, c: (0, 0)),
            pl.BlockSpec((1, SSM_WIDTH), lambda b, c: (0, 0)),
        ]
        args += [y_fwd, z_act, d_exp, norm_g]
    return pl.pallas_call(
        functools.partial(_ssd_kernel, reverse=reverse, finalize=finalize),
        grid=(bt, nc),
        in_specs=in_specs,
        out_specs=pl.BlockSpec((None, t, SSM_WIDTH), lambda b, c: (b, cidx(c), 0)),
        out_shape=jax.ShapeDtypeStruct((bt, seq, SSM_WIDTH), BF16 if finalize else F32),
        scratch_shapes=[pltpu.VMEM((SSM_STATE, SSM_WIDTH), F32)],
        compiler_params=_params(("parallel", "arbitrary")),
        name="ssd_bwd" if reverse else "ssd_fwd",
    )(*args)


def _hy_filter_kernel(fr_ref, w1t_ref, w1c_ref, w1s_ref, b1_ref, w2_ref, b2_ref, w3_ref, b3_ref, fq_ref,
                      w4_ref, dl_ref, ks_ref, kd_ref, *, n):
    tl = ks_ref.shape[0]
    pos = (pl.program_id(0) * tl + lax.broadcasted_iota(jnp.int32, (tl, 1), 0)).astype(F32)
    tt = pos / float(n - 1)
    arg = fr_ref[...] * ((2.0 * math.pi / n) * pos)
    fq = fq_ref[...]
    dot = functools.partial(jnp.dot, preferred_element_type=F32)
    h = tt * w1t_ref[...] + dot(jnp.cos(arg), w1c_ref[...]) + dot(-jnp.sin(arg), w1s_ref[...]) + b1_ref[...]
    h = jnp.sin(fq * h)
    h = jnp.sin(fq * (dot(h, w2_ref[...]) + b2_ref[...]))
    h = jnp.sin(fq * (dot(h, w3_ref[...]) + b3_ref[...]))
    k = dot(h, w4_ref[...])
    win = jnp.exp(-tt * dl_ref[...])
    kf = k[:, :HY_WIDTH] * win
    kb = jnp.where(pos == 0.0, 0.0, k[:, HY_WIDTH:] * win)
    ks_ref[...] = (kf + kb).astype(ks_ref.dtype)
    kd_ref[...] = (kf - kb).astype(kd_ref.dtype)


def _hy_filters(n, w1, b1, w2, b2, w3, b3, freq, w4):
    tl = _tile(n, 256)
    fr = jnp.linspace(1e-4, HY_BANDS - 1, HY_BANDS, dtype=F32).reshape(1, HY_BANDS)
    min_decay = math.log(HY_TARGET) / HY_SLOW_DECAY
    max_decay = math.log(HY_TARGET) / HY_FAST_DECAY
    deltas = jnp.abs(jnp.linspace(min_decay, max_decay, HY_WIDTH, dtype=F32)).reshape(1, HY_WIDTH)
    w1 = w1.astype(F32)
    small = [fr, w1[0:1], w1[1:1 + HY_BANDS], w1[1 + HY_BANDS:], b1.reshape(1, -1), w2, b2.reshape(1, -1),
             w3, b3.reshape(1, -1), freq.reshape(1, -1), w4, deltas]
    small = [s.astype(F32) for s in small]
    out_spec = pl.BlockSpec((tl, HY_WIDTH), lambda i: (i, 0))
    return pl.pallas_call(
        functools.partial(_hy_filter_kernel, n=n),
        grid=(n // tl,),
        in_specs=[pl.BlockSpec(s.shape, lambda i: (0, 0)) for s in small],
        out_specs=[out_spec, out_spec],
        out_shape=[jax.ShapeDtypeStruct((n, HY_WIDTH), BF16)] * 2,
        compiler_params=_params(("parallel",)),
        name="hy_filter",
    )(*small)


def _dft_gen_kernel(c_ref, s_ref, *, n, inverse):
    tm, tn = c_ref.shape
    r = pl.program_id(0) * tm + lax.broadcasted_iota(jnp.int32, (tm, tn), 0)
    c = pl.program_id(1) * tn + lax.broadcasted_iota(jnp.int32, (tm, tn), 1)
    m = ((2 * c + 1) * r) if inverse else ((2 * r + 1) * c)
    m = jnp.bitwise_and(m, 4 * n - 1)
    m = jnp.where(m >= 2 * n, m - 4 * n, m)
    ang = m.astype(F32) * (math.pi / (2 * n))
    scale = (1.0 / n) if inverse else 1.0
    c_ref[...] = (jnp.cos(ang) * scale).astype(c_ref.dtype)
    s_ref[...] = (jnp.sin(ang) * (-scale)).astype(s_ref.dtype)


def _dft_matrices(n, inverse):
    tm, tn = _tile(n, 256), _tile(n, 512)
    spec = pl.BlockSpec((tm, tn), lambda i, j: (i, j))
    return pl.pallas_call(
        functools.partial(_dft_gen_kernel, n=n, inverse=inverse),
        grid=(n // tm, n // tn),
        out_specs=[spec, spec],
        out_shape=[jax.ShapeDtypeStruct((n, n), BF16)] * 2,
        compiler_params=_params(("parallel", "parallel")),
        name="dft_gen_inv" if inverse else "dft_gen_fwd",
    )()


def _hy_fwd_kernel(co_ref, sn_ref, w_ref, kre_ref, kim_ref, yre_ref, yim_ref):
    w = w_ref[...]
    wre = jnp.dot(co_ref[...], w, preferred_element_type=F32)
    wim = jnp.dot(sn_ref[...], w, preferred_element_type=F32)
    kre, kim = kre_ref[...], kim_ref[...]
    yre_ref[...] = (wre * kre - wim * kim).astype(yre_ref.dtype)
    yim_ref[...] = (wre * kim + wim * kre).astype(yim_ref.dtype)


def _hy_fwd_call(co, sn, w, kre, kim):
    bt, n, width = w.shape
    tm, tn = _tile(n, 512), _tile(width, 512)
    fspec = pl.BlockSpec((tm, n), lambda i, b, j: (i, 0))
    kspec = pl.BlockSpec((tm, tn), lambda i, b, j: (i, j))
    ospec = pl.BlockSpec((None, tm, tn), lambda i, b, j: (b, i, j))
    return pl.pallas_call(
        _hy_fwd_kernel,
        grid=(n // tm, bt, width // tn),
        in_specs=[fspec, fspec, pl.BlockSpec((None, n, tn), lambda i, b, j: (b, 0, j)), kspec, kspec],
        out_specs=[ospec, ospec],
        out_shape=[jax.ShapeDtypeStruct((bt, n, width), BF16)] * 2,
        compiler_params=_params(("parallel", "parallel", "parallel")),
        name="hy_dft_fwd",
    )(co, sn, w, kre, kim)


def _hy_inv_kernel(ct_ref, st_ref, yre_ref, yim_ref, x0_ref, w_ref, bias_ref, o_ref):
    yc = (jnp.dot(ct_ref[...], yre_ref[...], preferred_element_type=F32)
          + jnp.dot(st_ref[...], yim_ref[...], preferred_element_type=F32))
    o_ref[...] = (x0_ref[...].astype(F32) * (yc + w_ref[...].astype(F32) * bias_ref[...])).astype(o_ref.dtype)


def _hy_inv_call(ct, st, yre, yim, x0, w, bias):
    bt, n, width = w.shape
    tm, tn = _tile(n, 512), _tile(width, 512)
    fspec = pl.BlockSpec((tm, n), lambda i, b, j: (i, 0))
    yspec = pl.BlockSpec((None, n, tn), lambda i, b, j: (b, 0, j))
    tspec = pl.BlockSpec((None, tm, tn), lambda i, b, j: (b, i, j))
    return pl.pallas_call(
        _hy_inv_kernel,
        grid=(n // tm, bt, width // tn),
        in_specs=[fspec, fspec, yspec, yspec, tspec, tspec, pl.BlockSpec((1, tn), lambda i, b, j: (0, j))],
        out_specs=tspec,
        out_shape=jax.ShapeDtypeStruct((bt, n, width), BF16),
        compiler_params=_params(("parallel", "parallel", "parallel")),
        name="hy_dft_inv",
    )(ct, st, yre, yim, x0, w, bias.reshape(1, width).astype(F32))


def _sgu_kernel(x_ref, lng_ref, lnb_ref, ws_ref, bse_ref, o_ref):
    tl = x_ref.shape[0]
    v = x_ref[:, SG_WIDTH:].astype(F32)
    mu = jnp.mean(v, axis=-1, keepdims=True)
    vc = v - mu
    var = jnp.mean(vc * vc, axis=-1, keepdims=True)
    vn = (vc * lax.rsqrt(var + LN_EPS) * lng_ref[...] + lnb_ref[...]).astype(BF16)
    for g in range(SG_GROUPS):
        cols = slice(g * SG_GROUP_DIM, (g + 1) * SG_GROUP_DIM)
        wsg = ws_ref[g]
        for k in range(tl // SG_CHUNK):
            rows = slice(k * SG_CHUNK, (k + 1) * SG_CHUNK)
            mixed = jnp.dot(wsg, vn[rows, cols], preferred_element_type=F32) + bse_ref[:, cols]
            o_ref[rows, cols] = (x_ref[rows, cols].astype(F32) * mixed).astype(o_ref.dtype)


def _sgu_call(sg_act, ln_g, ln_b, ws, bs):
    bt, seq, _ = sg_act.shape
    tl = _tile(seq, 2 * SG_CHUNK)
    bs_exp = jnp.repeat(bs.astype(F32).T, SG_GROUP_DIM, axis=1)
    return pl.pallas_call(
        _sgu_kernel,
        grid=(bt, seq // tl),
        in_specs=[pl.BlockSpec((None, tl, SG_IN), lambda b, i: (b, i, 0)),
                  pl.BlockSpec((1, SG_WIDTH), lambda b, i: (0, 0)),
                  pl.BlockSpec((1, SG_WIDTH), lambda b, i: (0, 0)),
                  pl.BlockSpec((SG_GROUPS, SG_CHUNK, SG_CHUNK), lambda b, i: (0, 0, 0)),
                  pl.BlockSpec((SG_CHUNK, SG_WIDTH), lambda b, i: (0, 0))],
        out_specs=pl.BlockSpec((None, tl, SG_WIDTH), lambda b, i: (b, i, 0)),
        out_shape=jax.ShapeDtypeStruct((bt, seq, SG_WIDTH), BF16),
        compiler_params=_params(("parallel", "parallel")),
        name="sgu",
    )(sg_act, ln_g.reshape(1, -1).astype(F32), ln_b.reshape(1, -1).astype(F32), ws.astype(BF16), bs_exp)


def _merge_kernel(ym_ref, yh_ref, yg_ref, w0_ref, w1_ref, w2_ref, g0_ref, g1_ref, g2_ref, o_ref):
    dot = functools.partial(jnp.dot, preferred_element_type=F32)
    tm = o_ref.shape[0]
    pm = min(tm, MM_PIECE_ROWS)
    for p in range(tm // pm):
        rows = slice(p * pm, (p + 1) * pm)
        acc = g0_ref[rows, :].astype(F32) * dot(ym_ref[rows, :], w0_ref[...])
        acc = acc + g1_ref[rows, :].astype(F32) * dot(yh_ref[rows, :], w1_ref[...])
        acc = acc + g2_ref[rows, :].astype(F32) * dot(yg_ref[rows, :], w2_ref[...])
        o_ref[rows, :] = acc.astype(o_ref.dtype)


def _merge_call(y_m, y_h, y_g, w0, w1, w2, gates):
    m = y_m.shape[0]
    tm, tn = _tile(m, 512), 512
    nb = D_MODEL // tn
    aspec = lambda width: pl.BlockSpec((tm, width), lambda i, j: (i, 0))
    wspec = lambda width: pl.BlockSpec((width, tn), lambda i, j: (0, j))
    gspec = lambda br: pl.BlockSpec((tm, tn), lambda i, j: (i, br * nb + j))
    return pl.pallas_call(
        _merge_kernel,
        grid=(m // tm, nb),
        in_specs=[aspec(SSM_WIDTH), aspec(HY_WIDTH), aspec(SG_WIDTH), wspec(SSM_WIDTH), wspec(HY_WIDTH),
                  wspec(SG_WIDTH), gspec(0), gspec(1), gspec(2)],
        out_specs=pl.BlockSpec((tm, tn), lambda i, j: (i, j)),
        out_shape=jax.ShapeDtypeStruct((m, D_MODEL), BF16),
        compiler_params=_params(("parallel", "parallel")),
        name="merge",
    )(y_m, y_h, y_g, w0, w1, w2, gates, gates, gates)


def _prep_layer(l, p):
    w_in = p["w_in"][l]
    o_xbc, o_dt, o_hy = SSM_WIDTH, SSM_WIDTH + SSM_XBC, SSM_IN
    o_sg, o_gate = SSM_IN + HY_IN, SSM_IN + HY_IN + SG_IN
    cut = lambda a, b: w_in[:, a:b].astype(BF16)
    w_br = p["w_br"][l]
    heads = jnp.arange(2 * SSM_HEADS)[:, None]
    chan_head = (jnp.arange(SSM_WIDTH) // SSM_HEAD_DIM)[None, :]
    return dict(
        norm1_g=p["norm1_g"][l], norm2_g=p["norm2_g"][l],
        w_z=cut(0, o_xbc), w_xbc=cut(o_xbc, o_dt), w_dt=cut(o_dt, o_hy), w_hy=cut(o_hy, o_sg),
        w_sg=cut(o_sg, o_gate), w_gate=cut(o_gate, o_gate + GATE_IN),
        b_gate=p["b_gate"][l].reshape(1, -1).astype(F32),
        ssm_conv_w=p["ssm_conv_w"][l], ssm_conv_b=p["ssm_conv_b"][l],
        dt_bias=p["ssm_dt_bias"][l].reshape(1, -1).astype(F32),
        a_log=p["ssm_a_log"][l].reshape(1, -1).astype(F32),
        e_fwd=(heads == chan_head).astype(BF16), e_bwd=(heads == chan_head + SSM_HEADS).astype(BF16),
        d_exp=jnp.repeat(p["ssm_d"][l].astype(F32), SSM_HEAD_DIM).reshape(1, -1),
        ssm_norm_g=p["ssm_norm_g"][l].reshape(1, -1).astype(F32),
        hy_conv_w=p["hy_conv_w"][l], hy_conv_b=p["hy_conv_b"][l],
        hy_mlp=tuple(p[k][l] for k in ("hy_w1", "hy_b1", "hy_w2", "hy_b2", "hy_w3", "hy_b3", "hy_freq", "hy_w4")),
        hy_bias=p["hy_bias"][l],
        sg_ln_g=p["sg_ln_g"][l], sg_ln_b=p["sg_ln_b"][l], sg_ws=p["sg_ws"][l], sg_bs=p["sg_bs"][l],
        w_br0=w_br[:SSM_WIDTH].astype(BF16), w_br1=w_br[SSM_WIDTH:SSM_WIDTH + HY_WIDTH].astype(BF16),
        w_br2=w_br[SSM_WIDTH + HY_WIDTH:].astype(BF16),
        w_out=p["w_out"][l].astype(BF16), w_up=p["w_up"][l].astype(BF16),
        ffn_conv_w=p["ffn_conv_w"][l], ffn_conv_b=p["ffn_conv_b"][l], w_down=p["w_down"][l].astype(BF16),
    )


def _ssd_branch(lw, h, dt_raw, z_act, bt, seq, tm=1024):
    xbc_act = _mm_conv(h, lw["w_xbc"], lw["ssm_conv_w"], lw["ssm_conv_b"], seq=seq, col_offsets=(0,),
                       width=SSM_XBC, combine=lambda c: (_silu(c),), n_out=1, tm=tm, tn=512,
                       name="in_xbc_conv")[0].reshape(bt, seq, SSM_XBC)
    dt3 = dt_raw.reshape(bt, seq, 2 * SSM_HEADS)
    y_fwd = _ssd_call(xbc_act, dt3, lw["dt_bias"], lw["a_log"], lw["e_fwd"], reverse=False)
    y_m = _ssd_call(xbc_act, dt3, lw["dt_bias"], lw["a_log"], lw["e_bwd"], reverse=True, y_fwd=y_fwd,
                    z_act=z_act.reshape(bt, seq, SSM_WIDTH), d_exp=lw["d_exp"], norm_g=lw["ssm_norm_g"])
    return y_m.reshape(bt * seq, SSM_WIDTH)


def _hyena_branch(lw, h, dft, bt, seq, tm=1024):
    co, sn, ct, st = dft
    x0, w = _mm_conv(h, lw["w_hy"], lw["hy_conv_w"], lw["hy_conv_b"], seq=seq,
                     col_offsets=(0, HY_WIDTH, 2 * HY_WIDTH), width=HY_WIDTH,
                     combine=lambda c0, c1, c2: (c0, c2 * c1), n_out=2, tm=tm, tn=256, name="in_hy_conv")
    x0 = x0.reshape(bt, seq, HY_WIDTH)
    w = w.reshape(bt, seq, HY_WIDTH)
    ks, kd = _hy_filters(seq, *lw["hy_mlp"])
    kre = _matmul(co, ks, out_dtype=F32, name="hy_kre", tm=512, tn=512)
    kim = _matmul(sn, kd, out_dtype=F32, name="hy_kim", tm=512, tn=512)
    yre, yim = _hy_fwd_call(co, sn, w, kre, kim)
    y_h = _hy_inv_call(ct, st, yre, yim, x0, w, lw["hy_bias"])
    return y_h.reshape(bt * seq, HY_WIDTH)


def _layer(x, lw, dft, bt, seq, tm=1024):
    h = _rmsnorm(x, lw["norm1_g"], BF16)
    z_act = _matmul(h, lw["w_z"], out_dtype=BF16, name="in_z", epilogue=_ep_silu)
    dt_raw = _matmul(h, lw["w_dt"], out_dtype=F32, name="in_dt")
    sg_act = _matmul(h, lw["w_sg"], out_dtype=BF16, name="in_sg", epilogue=_ep_gelu)
    gates = _matmul(h, lw["w_gate"], out_dtype=BF16, name="in_gate", epilogue=_ep_gate, rows=(lw["b_gate"],))
    y_m = _ssd_branch(lw, h, dt_raw, z_act, bt, seq, tm)
    y_h = _hyena_branch(lw, h, dft, bt, seq, tm)
    y_g = _sgu_call(sg_act.reshape(bt, seq, SG_IN), lw["sg_ln_g"], lw["sg_ln_b"], lw["sg_ws"],
                    lw["sg_bs"]).reshape(bt * seq, SG_WIDTH)
    merged = _merge_call(y_m, y_h, y_g, lw["w_br0"], lw["w_br1"], lw["w_br2"], gates)
    x = _matmul(merged, lw["w_out"], out_dtype=F32, name="out_proj", epilogue=_ep_residual, tiles=(x,))
    h2 = _rmsnorm(x, lw["norm2_g"], BF16)
    act = _mm_conv(h2, lw["w_up"], lw["ffn_conv_w"], lw["ffn_conv_b"], seq=seq, col_offsets=(0, D_FF),
                   width=D_FF, combine=lambda g, v: (_silu(g) * v,), n_out=1, tm=tm, tn=256,
                   name="ffn_up_conv")[0]
    return _matmul(act, lw["w_down"], out_dtype=F32, name="ffn_down",
                   epilogue=_ep_residual, tiles=(x,), tm=512, tn=512)


def _trunk(x, layers, normf_g):
    bt, seq, d = x.shape
    dft = (*_dft_matrices(seq, inverse=False), *_dft_matrices(seq, inverse=True))
    xf = x.reshape(bt * seq, d)
    for lw in layers:
        xf = _layer(xf, lw, dft, bt, seq)
    return _rmsnorm(xf, normf_g, F32).reshape(bt, seq, d)


def kernel(x_prompt, x_sample, norm1_g, w_in, b_gate, ssm_conv_w, ssm_conv_b, ssm_dt_bias, ssm_a_log, ssm_d,
           ssm_norm_g, hy_conv_w, hy_conv_b, hy_w1, hy_b1, hy_w2, hy_b2, hy_w3, hy_b3, hy_freq, hy_w4, hy_bias,
           sg_ln_g, sg_ln_b, sg_ws, sg_bs, w_br, w_out, norm2_g, w_up, ffn_conv_w, ffn_conv_b, w_down, normf_g):
    p = dict(norm1_g=norm1_g, w_in=w_in, b_gate=b_gate, ssm_conv_w=ssm_conv_w, ssm_conv_b=ssm_conv_b,
             ssm_dt_bias=ssm_dt_bias, ssm_a_log=ssm_a_log, ssm_d=ssm_d, ssm_norm_g=ssm_norm_g,
             hy_conv_w=hy_conv_w, hy_conv_b=hy_conv_b, hy_w1=hy_w1, hy_b1=hy_b1, hy_w2=hy_w2, hy_b2=hy_b2,
             hy_w3=hy_w3, hy_b3=hy_b3, hy_freq=hy_freq, hy_w4=hy_w4, hy_bias=hy_bias, sg_ln_g=sg_ln_g,
             sg_ln_b=sg_ln_b, sg_ws=sg_ws, sg_bs=sg_bs, w_br=w_br, w_out=w_out, norm2_g=norm2_g, w_up=w_up,
             ffn_conv_w=ffn_conv_w, ffn_conv_b=ffn_conv_b, w_down=w_down)
    layers = [_prep_layer(l, p) for l in range(w_in.shape[0])]
    return (_trunk(x_prompt, layers, normf_g), _trunk(x_sample, layers, normf_g))
```

```python
import functools
import math

import jax
import jax.numpy as jnp
from jax import lax
from jax.experimental import pallas as pl
from jax.experimental.pallas import tpu as pltpu

F32 = jnp.float32
BF16 = jnp.bfloat16

D_MODEL = 4096
SSM_WIDTH = D_MODEL
SSM_HEAD_DIM = 64
SSM_HEADS = SSM_WIDTH // SSM_HEAD_DIM
SSM_GROUPS = 8
SSM_STATE = 128
SSM_CONV = 5
SSD_CHUNK = 128
SSM_GN = SSM_GROUPS * SSM_STATE
SSM_XBC = SSM_WIDTH + 2 * SSM_GN
SSM_IN = SSM_WIDTH + SSM_XBC + 2 * SSM_HEADS
SSM_GROUP_WIDTH = SSM_WIDTH // SSM_GROUPS
HY_WIDTH = D_MODEL // 2
HY_SHORT = 3
HY_EMB = 33
HY_BANDS = (HY_EMB - 1) // 2
HY_ORDER = 64
HY_FAST_DECAY = 0.3
HY_SLOW_DECAY = 1.5
HY_TARGET = 1e-2
HY_IN = 3 * HY_WIDTH
SG_WIDTH = D_MODEL // 2
SG_CHUNK = 128
SG_GROUPS = 16
SG_GROUP_DIM = SG_WIDTH // SG_GROUPS
SG_IN = 2 * SG_WIDTH
N_BRANCH = 3
GATE_IN = N_BRANCH * D_MODEL
D_FF = 256 * math.ceil(8 * D_MODEL / 3 / 256)
D_FF_PADDED = 512 * math.ceil(D_FF / 512)
FFN_CONV = 3
RMS_EPS = 1e-6
LN_EPS = 1e-5

V7X_SCOPED_VMEM_BYTES = 56 * 1024 * 1024
LANES = 128
SUBLANES_BF16 = 16
HALO_KEEP = 8
MM_CONV_PIECE_ROWS = 128
MM_PIECE_ROWS = 256


def _tile(dim, pref):
    t = pref
    while dim % t:
        t //= 2
    return t


def _params(semantics):
    return pltpu.CompilerParams(dimension_semantics=semantics, vmem_limit_bytes=V7X_SCOPED_VMEM_BYTES)


def _silu(x):
    return x * jax.nn.sigmoid(x)


def _row_scale(ssq):
    return lax.rsqrt(ssq[:, 0:1] * (1.0 / D_MODEL) + RMS_EPS)


def _rmsnorm_kernel(x_ref, g_ref, o_ref):
    x = x_ref[...]
    ms = jnp.mean(x * x, axis=-1, keepdims=True)
    o_ref[...] = (x * lax.rsqrt(ms + RMS_EPS) * g_ref[...]).astype(o_ref.dtype)


def _rmsnorm(x, g, out_dtype):
    m, d = x.shape
    tr = _tile(m, 256)
    return pl.pallas_call(
        _rmsnorm_kernel,
        grid=(m // tr,),
        in_specs=[pl.BlockSpec((tr, d), lambda i: (i, 0)), pl.BlockSpec((1, d), lambda i: (0, 0))],
        out_specs=pl.BlockSpec((tr, d), lambda i: (i, 0)),
        out_shape=jax.ShapeDtypeStruct((m, d), out_dtype),
        compiler_params=_params(("parallel",)),
        name="rmsnorm",
    )(x, g.reshape(1, d).astype(F32))


def _norm_stats_kernel(x_ref, xb_ref, ssq_ref):
    x = x_ref[...]
    xb_ref[...] = x.astype(xb_ref.dtype)
    ssq_ref[...] = jnp.broadcast_to(jnp.sum(x * x, axis=-1, keepdims=True), ssq_ref.shape)


def _norm_stats(x):
    m, d = x.shape
    tr = _tile(m, 256)
    return pl.pallas_call(
        _norm_stats_kernel,
        grid=(m // tr,),
        in_specs=[pl.BlockSpec((tr, d), lambda i: (i, 0))],
        out_specs=[pl.BlockSpec((tr, d), lambda i: (i, 0)), pl.BlockSpec((tr, LANES), lambda i: (i, 0))],
        out_shape=[jax.ShapeDtypeStruct((m, d), BF16), jax.ShapeDtypeStruct((m, LANES), F32)],
        compiler_params=_params(("parallel",)),
        name="norm_stats",
    )(x)


def _mm_kernel(a_ref, b_ref, *rest, n_rows, n_tiles, has_scale, epilogue, norm_out):
    pos = 0
    ssq_in_ref = rest[pos] if has_scale else None
    pos += int(has_scale)
    row_refs = rest[pos:pos + n_rows]
    tile_refs = rest[pos + n_rows:pos + n_rows + n_tiles]
    outs = rest[pos + n_rows + n_tiles:]
    o_ref = outs[0]
    j = pl.program_id(1)
    tm = a_ref.shape[0]
    pm = min(tm, MM_PIECE_ROWS)
    for p in range(tm // pm):
        rows = slice(p * pm, (p + 1) * pm)
        acc = jnp.dot(a_ref[rows, :], b_ref[...], preferred_element_type=F32)
        if has_scale:
            acc = acc * _row_scale(ssq_in_ref[rows, :])
        if epilogue is not None:
            acc = epilogue(acc, *[r[...] for r in row_refs], *[t[rows, :] for t in tile_refs])
        o_ref[rows, :] = acc.astype(o_ref.dtype)
        if norm_out:
            xb_ref, ssq_ref = outs[1:]
            xb_ref[rows, :] = acc.astype(xb_ref.dtype)
            part = jnp.broadcast_to(jnp.sum(acc * acc, axis=-1, keepdims=True), (pm, LANES))
            ssq_ref[rows, :] = jnp.where(j == 0, part, ssq_ref[rows, :] + part)


def _matmul(a, b, *, out_dtype, name, epilogue=None, rows=(), tiles=(), row_ssq=None, norm_out=False,
            tm=1024, tn=1024):
    m, k = a.shape
    _, n = b.shape
    tm, tn = _tile(m, tm), _tile(n, tn)
    has_scale = row_ssq is not None
    in_specs = [pl.BlockSpec((tm, k), lambda i, j: (i, 0)), pl.BlockSpec((k, tn), lambda i, j: (0, j))]
    in_specs += [pl.BlockSpec((tm, LANES), lambda i, j: (i, 0))] if has_scale else []
    in_specs += [pl.BlockSpec((1, tn), lambda i, j: (0, j)) for _ in rows]
    in_specs += [pl.BlockSpec((tm, tn), lambda i, j: (i, j)) for _ in tiles]
    out_specs = [pl.BlockSpec((tm, tn), lambda i, j: (i, j))]
    out_shape = [jax.ShapeDtypeStruct((m, n), out_dtype)]
    if norm_out:
        out_specs += [pl.BlockSpec((tm, tn), lambda i, j: (i, j)), pl.BlockSpec((tm, LANES), lambda i, j: (i, 0))]
        out_shape += [jax.ShapeDtypeStruct((m, n), BF16), jax.ShapeDtypeStruct((m, LANES), F32)]
    res = pl.pallas_call(
        functools.partial(_mm_kernel, n_rows=len(rows), n_tiles=len(tiles), has_scale=has_scale,
                          epilogue=epilogue, norm_out=norm_out),
        grid=(m // tm, n // tn),
        in_specs=in_specs,
        out_specs=out_specs,
        out_shape=out_shape,
        compiler_params=_params(("parallel", "arbitrary" if norm_out else "parallel")),
        name=name,
    )(a, b, *([row_ssq] if has_scale else []), *rows, *tiles)
    return res if norm_out else res[0]


def _ep_silu(acc):
    return _silu(acc)


def _ep_gelu(acc):
    return 0.5 * acc * (1.0 + lax.erf(acc * (1.0 / math.sqrt(2.0))))


def _ep_gate(acc, bias):
    return jax.nn.sigmoid(acc + bias)


def _ep_residual(acc, res):
    return acc + res


def _mm_conv_kernel(a_ref, ssq_ref, *refs, n_groups, ksize, n_row_tiles, tiles_per_seq, combine):
    w_refs = refs[:n_groups]
    cw_refs = refs[n_groups:2 * n_groups]
    cb_refs = refs[2 * n_groups:3 * n_groups]
    out_refs = refs[3 * n_groups:-2]
    p_ref, carry_ref = refs[-2:]
    s = pl.program_id(0)
    tm = a_ref.shape[0]
    pad = ksize // 2

    @pl.when(s == 0)
    def _():
        p_ref[...] = jnp.zeros_like(p_ref)
        carry_ref[...] = jnp.zeros_like(carry_ref)

    prev_tile = jnp.maximum(s - 1, 0) % n_row_tiles
    has_prev = jnp.where(prev_tile % tiles_per_seq != 0, 1.0, 0.0)
    has_next = jnp.where(prev_tile % tiles_per_seq != tiles_per_seq - 1, 1.0, 0.0)
    pm = min(tm, MM_CONV_PIECE_ROWS)
    n_pieces = tm // pm
    cur = [[None] * n_pieces for _ in range(n_groups)]
    for k in range(n_pieces):
        rows = slice(k * pm, (k + 1) * pm)
        a = a_ref[rows, :]
        scale = _row_scale(ssq_ref[rows, :])
        for g in range(n_groups):
            cur[g][k] = jnp.dot(a, w_refs[g][...], preferred_element_type=F32) * scale
        conv = []
        for g in range(n_groups):
            head = carry_ref[g] * has_prev if k == 0 else p_ref[g, k * pm - HALO_KEEP:k * pm, :]
            tail = (cur[g][0][:HALO_KEEP] * has_next if k == n_pieces - 1
                    else p_ref[g, (k + 1) * pm:(k + 1) * pm + HALO_KEEP, :])
            ext = jnp.concatenate([head, p_ref[g, rows, :], tail], axis=0)
            w = cw_refs[g][...]
            acc = cb_refs[g][...]
            for j in range(ksize):
                off = HALO_KEEP - pad + j
                acc = acc + w[j:j + 1, :] * ext[off:off + pm]
            conv.append(acc)
        for o_ref, val in zip(out_refs, combine(*conv)):
            o_ref[rows, :] = val.astype(o_ref.dtype)
    for g in range(n_groups):
        carry_ref[g] = p_ref[g, tm - HALO_KEEP:, :]
        for k in range(n_pieces):
            p_ref[g, k * pm:(k + 1) * pm, :] = cur[g][k]


def _mm_conv(a, row_ssq, groups, *, seq, width, combine, n_out, tm, tn, name):
    t, k = a.shape
    ksize = groups[0][1].shape[0]
    tm, tn = _tile(seq, tm), _tile(width, tn)
    n_row_tiles, n_col_tiles = t // tm, width // tn
    steps = n_row_tiles * n_col_tiles
    n_groups = len(groups)
    cur_tile = lambda s: jnp.minimum(s, steps - 1)
    fin_tile = lambda s: jnp.maximum(s - 1, 0)
    offs = [g[3] // tn for g in groups]
    in_specs = [pl.BlockSpec((tm, k), lambda s: (cur_tile(s) % n_row_tiles, 0)),
                pl.BlockSpec((tm, LANES), lambda s: (cur_tile(s) % n_row_tiles, 0))]
    in_specs += [pl.BlockSpec((k, tn), lambda s, o=o: (0, o + cur_tile(s) // n_row_tiles)) for o in offs]
    in_specs += [pl.BlockSpec((ksize, tn), lambda s, o=o: (0, o + fin_tile(s) // n_row_tiles)) for o in offs]
    in_specs += [pl.BlockSpec((1, tn), lambda s, o=o: (0, o + fin_tile(s) // n_row_tiles)) for o in offs]
    out_spec = pl.BlockSpec((tm, tn), lambda s: (fin_tile(s) % n_row_tiles, fin_tile(s) // n_row_tiles))
    return pl.pallas_call(
        functools.partial(_mm_conv_kernel, n_groups=n_groups, ksize=ksize, n_row_tiles=n_row_tiles,
                          tiles_per_seq=seq // tm, combine=combine),
        grid=(steps + 1,),
        in_specs=in_specs,
        out_specs=[out_spec] * n_out,
        out_shape=[jax.ShapeDtypeStruct((t, width), BF16)] * n_out,
        scratch_shapes=[pltpu.VMEM((n_groups, tm, tn), F32), pltpu.VMEM((n_groups, HALO_KEEP, tn), F32)],
        compiler_params=_params(("arbitrary",)),
        name=name,
    )(a, row_ssq, *[g[0] for g in groups], *[g[1].astype(F32) for g in groups],
      *[g[2].reshape(1, -1).astype(F32) for g in groups])


def _split3(x):
    hi = x.astype(BF16)
    r1 = x - hi.astype(F32)
    mid = r1.astype(BF16)
    lo = (r1 - mid.astype(F32)).astype(BF16)
    return hi, mid, lo


def _ssd_kernel(*refs, reverse, finalize):
    if finalize:
        (xs_ref, b_ref, c_ref, dtr_ref, dtb_ref, alog_ref, e_ref,
         yf_ref, z_ref, dexp_ref, ng_ref, o_ref, h_ref) = refs
    else:
        xs_ref, b_ref, c_ref, dtr_ref, dtb_ref, alog_ref, e_ref, o_ref, h_ref = refs
    t = SSD_CHUNK
    head0 = SSM_HEADS if reverse else 0

    @pl.when(pl.program_id(1) == 0)
    def _():
        h_ref[...] = jnp.zeros_like(h_ref)

    dt = jax.nn.softplus(dtr_ref[...] + dtb_ref[...])
    da = dt * (-jnp.exp(alog_ref[...]))
    row = lax.broadcasted_iota(jnp.int32, (t, t), 0)
    col = lax.broadcasted_iota(jnp.int32, (t, t), 1)
    if reverse:
        tri = (row > col).astype(BF16)
        tri_t = (col > row).astype(BF16)
        mask = col >= row
    else:
        tri = (row >= col).astype(BF16)
        tri_t = (col >= row).astype(BF16)
        mask = row >= col
    pieces = _split3(da)
    p = sum(jnp.dot(tri, x, preferred_element_type=F32) for x in pieces)
    p_t = sum(lax.dot_general(x, tri_t, (((0,), (0,)), ((), ())), preferred_element_type=F32)
              for x in pieces)
    total = jnp.sum(da, axis=0, keepdims=True)
    if reverse:
        dd = dt * jnp.exp(p)
        ea = jnp.exp(total - p)
    else:
        dd = dt * jnp.exp(total - p)
        ea = jnp.exp(p)
    q = jnp.concatenate([dt, dd, ea], axis=0).astype(BF16)
    etot = _split3(jnp.broadcast_to(jnp.exp(total), (SUBLANES_BF16, 2 * SSM_HEADS)))
    lane = lax.broadcasted_iota(jnp.int32, (t, 4 * SSM_HEAD_DIM), 1)

    for g in range(SSM_GROUPS):
        cols = slice(g * SSM_GROUP_WIDTH, (g + 1) * SSM_GROUP_WIDTH)
        scols = slice(g * SSM_STATE, (g + 1) * SSM_STATE)
        eg = e_ref[:, cols]
        ex = jnp.dot(q, eg, preferred_element_type=F32)
        dt_e, dd_e, ea_e = ex[0:t], ex[t:2 * t], ex[2 * t:3 * t]
        etot_e = sum(jnp.dot(x, eg, preferred_element_type=F32) for x in etot)[0:1]
        xs = xs_ref[:, cols].astype(F32)
        xc = (xs * dt_e).astype(BF16)
        xc2 = (xs * dd_e).astype(BF16)
        bg = b_ref[:, scols]
        cg = c_ref[:, scols]
        cb = lax.dot_general(cg, bg, (((1,), (1,)), ((), ())), preferred_element_type=F32)
        h_in = h_ref[:, cols]
        y = jnp.dot(cg, h_in.astype(BF16), preferred_element_type=F32) * ea_e
        s_new = lax.dot_general(bg, xc2, (((0,), (0,)), ((), ())), preferred_element_type=F32)
        h_ref[:, cols] = h_in * etot_e + s_new
        quads = []
        for qd in range(2):
            xq = xc[:, qd * 4 * SSM_HEAD_DIM:(qd + 1) * 4 * SSM_HEAD_DIM]
            yq = None
            for j in range(4):
                hc = head0 + g * 8 + qd * 4 + j
                pc = jnp.broadcast_to(p[:, hc:hc + 1], (t, t))
                pr = jnp.broadcast_to(p_t[hc:hc + 1, :], (t, t))
                seg = (pr - pc) if reverse else (pc - pr)
                dec = jnp.exp(jnp.where(mask, seg, -jnp.inf))
                m = (cb * dec).astype(BF16)
                in_head = (lane >= j * SSM_HEAD_DIM) & (lane < (j + 1) * SSM_HEAD_DIM)
                xm = jnp.where(in_head, xq, jnp.zeros_like(xq))
                d = jnp.dot(m, xm, preferred_element_type=F32)
                yq = d if yq is None else yq + d
            quads.append(yq)
        y = y + jnp.concatenate(quads, axis=1)
        if finalize:
            y = yf_ref[:, cols] + y + xs * dexp_ref[:, cols]
            y = y * z_ref[:, cols].astype(F32)
            ms = jnp.mean(y * y, axis=-1, keepdims=True)
            y = y * lax.rsqrt(ms + RMS_EPS) * ng_ref[:, cols]
        o_ref[:, cols] = y.astype(o_ref.dtype)


def _ssd_call(xbc_act, dt_raw, dt_bias, a_log, e_map, *, reverse, y_fwd=None, z_act=None, d_exp=None,
              norm_g=None):
    bt, seq, _ = xbc_act.shape
    t = SSD_CHUNK
    nc = seq // t
    finalize = y_fwd is not None
    cidx = (lambda c: nc - 1 - c) if reverse else (lambda c: c)
    gn_blocks = SSM_WIDTH // SSM_GN
    in_specs = [
        pl.BlockSpec((None, t, SSM_WIDTH), lambda b, c: (b, cidx(c), 0)),
        pl.BlockSpec((None, t, SSM_GN), lambda b, c: (b, cidx(c), gn_blocks)),
        pl.BlockSpec((None, t, SSM_GN), lambda b, c: (b, cidx(c), gn_blocks + 1)),
        pl.BlockSpec((None, t, 2 * SSM_HEADS), lambda b, c: (b, cidx(c), 0)),
        pl.BlockSpec((1, 2 * SSM_HEADS), lambda b, c: (0, 0)),
        pl.BlockSpec((1, 2 * SSM_HEADS), lambda b, c: (0, 0)),
        pl.BlockSpec((2 * SSM_HEADS, SSM_WIDTH), lambda b, c: (0, 0)),
    ]
    args = [xbc_act, xbc_act, xbc_act, dt_raw, dt_bias, a_log, e_map]
    if finalize:
        in_specs += [
            pl.BlockSpec((None, t, SSM_WIDTH), lambda b, c: (b, cidx(c), 0)),
            pl.BlockSpec((None, t, SSM_WIDTH), lambda b, c: (b, cidx(c), 0)),
            pl.BlockSpec((1, SSM_WIDTH), lambda b, c: (0, 0)),
            pl.BlockSpec((1, SSM_WIDTH), lambda b, c: (0, 0)),
        ]
        args += [y_fwd, z_act, d_exp, norm_g]
    return pl.pallas_call(
        functools.partial(_ssd_kernel, reverse=reverse, finalize=finalize),
        grid=(bt, nc),
        in_specs=in_specs,
        out_specs=pl.BlockSpec((None, t, SSM_WIDTH), lambda b, c: (b, cidx(c), 0)),
        out_shape=jax.ShapeDtypeStruct((bt, seq, SSM_WIDTH), BF16 if finalize else F32),
        scratch_shapes=[pltpu.VMEM((SSM_STATE, SSM_WIDTH), F32)],
        compiler_params=_params(("parallel", "arbitrary")),
        name="ssd_bwd" if reverse else "ssd_fwd",
    )(*args)


def _hy_filter_kernel(fr_ref, w1t_ref, w1c_ref, w1s_ref, b1_ref, w2_ref, b2_ref, w3_ref, b3_ref, fq_ref,
                      w4_ref, dl_ref, ks_ref, kd_ref, *, n):
    tl = ks_ref.shape[0]
    pos = (pl.program_id(0) * tl + lax.broadcasted_iota(jnp.int32, (tl, 1), 0)).astype(F32)
    tt = pos / float(n - 1)
    arg = fr_ref[...] * ((2.0 * math.pi / n) * pos)
    fq = fq_ref[...]
    dot = functools.partial(jnp.dot, preferred_element_type=F32)
    h = tt * w1t_ref[...] + dot(jnp.cos(arg), w1c_ref[...]) + dot(-jnp.sin(arg), w1s_ref[...]) + b1_ref[...]
    h = jnp.sin(fq * h)
    h = jnp.sin(fq * (dot(h, w2_ref[...]) + b2_ref[...]))
    h = jnp.sin(fq * (dot(h, w3_ref[...]) + b3_ref[...]))
    k = dot(h, w4_ref[...])
    win = jnp.exp(-tt * dl_ref[...])
    kf = k[:, :HY_WIDTH] * win
    kb = jnp.where(pos == 0.0, 0.0, k[:, HY_WIDTH:] * win)
    ks_ref[...] = (kf + kb).astype(ks_ref.dtype)
    kd_ref[...] = (kf - kb).astype(kd_ref.dtype)


def _hy_filters(n, w1, b1, w2, b2, w3, b3, freq, w4):
    tl = _tile(n, 256)
    fr = jnp.linspace(1e-4, HY_BANDS - 1, HY_BANDS, dtype=F32).reshape(1, HY_BANDS)
    min_decay = math.log(HY_TARGET) / HY_SLOW_DECAY
    max_decay = math.log(HY_TARGET) / HY_FAST_DECAY
    deltas = jnp.abs(jnp.linspace(min_decay, max_decay, HY_WIDTH, dtype=F32)).reshape(1, HY_WIDTH)
    w1 = w1.astype(F32)
    small = [fr, w1[0:1], w1[1:1 + HY_BANDS], w1[1 + HY_BANDS:], b1.reshape(1, -1), w2, b2.reshape(1, -1),
             w3, b3.reshape(1, -1), freq.reshape(1, -1), w4, deltas]
    small = [s.astype(F32) for s in small]
    out_spec = pl.BlockSpec((tl, HY_WIDTH), lambda i: (i, 0))
    return pl.pallas_call(
        functools.partial(_hy_filter_kernel, n=n),
        grid=(n // tl,),
        in_specs=[pl.BlockSpec(s.shape, lambda i: (0, 0)) for s in small],
        out_specs=[out_spec, out_spec],
        out_shape=[jax.ShapeDtypeStruct((n, HY_WIDTH), BF16)] * 2,
        compiler_params=_params(("parallel",)),
        name="hy_filter",
    )(*small)


def _dft_angles(m, n):
    m = jnp.bitwise_and(m, 4 * n - 1)
    m = jnp.where(m >= 2 * n, m - 4 * n, m)
    return m.astype(F32) * (math.pi / (2 * n))


def _dft_gen_kernel(c_ref, s_ref, cb_ref, sb_ref, *, n, inverse):
    tm, tn = c_ref.shape
    r0 = pl.program_id(1) * tm
    di = lax.broadcasted_iota(jnp.int32, (tm, tn), 0)
    c = pl.program_id(0) * tn + lax.broadcasted_iota(jnp.int32, (tm, tn), 1)
    c1 = pl.program_id(0) * tn + lax.broadcasted_iota(jnp.int32, (1, tn), 1)

    @pl.when(pl.program_id(1) == 0)
    def _():
        beta = _dft_angles((2 * c + 1) * di if inverse else 2 * di * c, n)
        cb_ref[...] = jnp.cos(beta)
        sb_ref[...] = jnp.sin(beta)

    alpha = _dft_angles((2 * c1 + 1) * r0 if inverse else (2 * r0 + 1) * c1, n)
    ca, sa = jnp.cos(alpha), jnp.sin(alpha)
    cb, sb = cb_ref[...], sb_ref[...]
    scale = (1.0 / n) if inverse else 1.0
    c_ref[...] = ((ca * cb - sa * sb) * scale).astype(c_ref.dtype)
    s_ref[...] = ((sa * cb + ca * sb) * (-scale)).astype(s_ref.dtype)


def _dft_matrices(n, inverse):
    tm, tn = _tile(n, 256), _tile(n, 512)
    spec = pl.BlockSpec((tm, tn), lambda j, i: (i, j))
    return pl.pallas_call(
        functools.partial(_dft_gen_kernel, n=n, inverse=inverse),
        grid=(n // tn, n // tm),
        out_specs=[spec, spec],
        out_shape=[jax.ShapeDtypeStruct((n, n), BF16)] * 2,
        scratch_shapes=[pltpu.VMEM((tm, tn), F32), pltpu.VMEM((tm, tn), F32)],
        compiler_params=_params(("parallel", "arbitrary")),
        name="dft_gen_inv" if inverse else "dft_gen_fwd",
    )()


def _hy_fwd_kernel(co_ref, sn_ref, w_ref, kre_ref, kim_ref, yre_ref, yim_ref):
    w = w_ref[...]
    wre = jnp.dot(co_ref[...], w, preferred_element_type=F32)
    wim = jnp.dot(sn_ref[...], w, preferred_element_type=F32)
    kre, kim = kre_ref[...], kim_ref[...]
    yre_ref[...] = (wre * kre - wim * kim).astype(yre_ref.dtype)
    yim_ref[...] = (wre * kim + wim * kre).astype(yim_ref.dtype)


def _hy_fwd_call(co, sn, w, kre, kim):
    bt, n, width = w.shape
    tm, tn = _tile(n, 512), _tile(width, 512)
    fspec = pl.BlockSpec((tm, n), lambda i, b, j: (i, 0))
    kspec = pl.BlockSpec((tm, tn), lambda i, b, j: (i, j))
    ospec = pl.BlockSpec((None, tm, tn), lambda i, b, j: (b, i, j))
    return pl.pallas_call(
        _hy_fwd_kernel,
        grid=(n // tm, bt, width // tn),
        in_specs=[fspec, fspec, pl.BlockSpec((None, n, tn), lambda i, b, j: (b, 0, j)), kspec, kspec],
        out_specs=[ospec, ospec],
        out_shape=[jax.ShapeDtypeStruct((bt, n, width), BF16)] * 2,
        compiler_params=_params(("parallel", "parallel", "parallel")),
        name="hy_dft_fwd",
    )(co, sn, w, kre, kim)


def _hy_inv_kernel(ct_ref, st_ref, yre_ref, yim_ref, x0_ref, w_ref, bias_ref, o_ref):
    yc = (jnp.dot(ct_ref[...], yre_ref[...], preferred_element_type=F32)
          + jnp.dot(st_ref[...], yim_ref[...], preferred_element_type=F32))
    o_ref[...] = (x0_ref[...].astype(F32) * (yc + w_ref[...].astype(F32) * bias_ref[...])).astype(o_ref.dtype)


def _hy_inv_call(ct, st, yre, yim, x0, w, bias):
    bt, n, width = w.shape
    tm, tn = _tile(n, 512), _tile(width, 512)
    fspec = pl.BlockSpec((tm, n), lambda i, b, j: (i, 0))
    yspec = pl.BlockSpec((None, n, tn), lambda i, b, j: (b, 0, j))
    tspec = pl.BlockSpec((None, tm, tn), lambda i, b, j: (b, i, j))
    return pl.pallas_call(
        _hy_inv_kernel,
        grid=(n // tm, bt, width // tn),
        in_specs=[fspec, fspec, yspec, yspec, tspec, tspec, pl.BlockSpec((1, tn), lambda i, b, j: (0, j))],
        out_specs=tspec,
        out_shape=jax.ShapeDtypeStruct((bt, n, width), BF16),
        compiler_params=_params(("parallel", "parallel", "parallel")),
        name="hy_dft_inv",
    )(ct, st, yre, yim, x0, w, bias.reshape(1, width).astype(F32))


def _sgu_kernel(x_ref, lng_ref, lnb_ref, ws_ref, bse_ref, o_ref):
    tl = x_ref.shape[0]
    v = x_ref[:, SG_WIDTH:].astype(F32)
    mu = jnp.mean(v, axis=-1, keepdims=True)
    vc = v - mu
    var = jnp.mean(vc * vc, axis=-1, keepdims=True)
    vn = (vc * lax.rsqrt(var + LN_EPS) * lng_ref[...] + lnb_ref[...]).astype(BF16)
    for g in range(SG_GROUPS):
        cols = slice(g * SG_GROUP_DIM, (g + 1) * SG_GROUP_DIM)
        wsg = ws_ref[g]
        for k in range(tl // SG_CHUNK):
            rows = slice(k * SG_CHUNK, (k + 1) * SG_CHUNK)
            mixed = jnp.dot(wsg, vn[rows, cols], preferred_element_type=F32) + bse_ref[:, cols]
            o_ref[rows, cols] = (x_ref[rows, cols].astype(F32) * mixed).astype(o_ref.dtype)


def _sgu_call(sg_act, ln_g, ln_b, ws, bs):
    bt, seq, _ = sg_act.shape
    tl = _tile(seq, 2 * SG_CHUNK)
    bs_exp = jnp.repeat(bs.astype(F32).T, SG_GROUP_DIM, axis=1)
    return pl.pallas_call(
        _sgu_kernel,
        grid=(bt, seq // tl),
        in_specs=[pl.BlockSpec((None, tl, SG_IN), lambda b, i: (b, i, 0)),
                  pl.BlockSpec((1, SG_WIDTH), lambda b, i: (0, 0)),
                  pl.BlockSpec((1, SG_WIDTH), lambda b, i: (0, 0)),
                  pl.BlockSpec((SG_GROUPS, SG_CHUNK, SG_CHUNK), lambda b, i: (0, 0, 0)),
                  pl.BlockSpec((SG_CHUNK, SG_WIDTH), lambda b, i: (0, 0))],
        out_specs=pl.BlockSpec((None, tl, SG_WIDTH), lambda b, i: (b, i, 0)),
        out_shape=jax.ShapeDtypeStruct((bt, seq, SG_WIDTH), BF16),
        compiler_params=_params(("parallel", "parallel")),
        name="sgu",
    )(sg_act, ln_g.reshape(1, -1).astype(F32), ln_b.reshape(1, -1).astype(F32), ws.astype(BF16), bs_exp)


def _merge_kernel(ym_ref, yh_ref, yg_ref, w0_ref, w1_ref, w2_ref, g0_ref, g1_ref, g2_ref, o_ref):
    dot = functools.partial(jnp.dot, preferred_element_type=F32)
    tm = o_ref.shape[0]
    pm = min(tm, MM_PIECE_ROWS)
    for p in range(tm // pm):
        rows = slice(p * pm, (p + 1) * pm)
        acc = g0_ref[rows, :].astype(F32) * dot(ym_ref[rows, :], w0_ref[...])
        acc = acc + g1_ref[rows, :].astype(F32) * dot(yh_ref[rows, :], w1_ref[...])
        acc = acc + g2_ref[rows, :].astype(F32) * dot(yg_ref[rows, :], w2_ref[...])
        o_ref[rows, :] = acc.astype(o_ref.dtype)


def _merge_call(y_m, y_h, y_g, w0, w1, w2, gates):
    m = y_m.shape[0]
    tm, tn = _tile(m, 512), 512
    nb = D_MODEL // tn
    aspec = lambda width: pl.BlockSpec((tm, width), lambda i, j: (i, 0))
    wspec = lambda width: pl.BlockSpec((width, tn), lambda i, j: (0, j))
    gspec = lambda br: pl.BlockSpec((tm, tn), lambda i, j: (i, br * nb + j))
    return pl.pallas_call(
        _merge_kernel,
        grid=(m // tm, nb),
        in_specs=[aspec(SSM_WIDTH), aspec(HY_WIDTH), aspec(SG_WIDTH), wspec(SSM_WIDTH), wspec(HY_WIDTH),
                  wspec(SG_WIDTH), gspec(0), gspec(1), gspec(2)],
        out_specs=pl.BlockSpec((tm, tn), lambda i, j: (i, j)),
        out_shape=jax.ShapeDtypeStruct((m, D_MODEL), BF16),
        compiler_params=_params(("parallel", "parallel")),
        name="merge",
    )(y_m, y_h, y_g, w0, w1, w2, gates, gates, gates)


def _prep_layer(l, p):
    n1 = p["norm1_g"][l].astype(F32)[:, None]
    n2 = p["norm2_g"][l].astype(F32)[:, None]
    w_in = p["w_in"][l]
    o_xbc, o_dt, o_hy = SSM_WIDTH, SSM_WIDTH + SSM_XBC, SSM_IN
    o_sg, o_gate = SSM_IN + HY_IN, SSM_IN + HY_IN + SG_IN
    cut = lambda a, b: (n1 * w_in[:, a:b]).astype(BF16)
    w_br = p["w_br"][l]
    heads = jnp.arange(2 * SSM_HEADS)[:, None]
    chan_head = (jnp.arange(SSM_WIDTH) // SSM_HEAD_DIM)[None, :]
    ff_pad = D_FF_PADDED - D_FF
    w_up = n2 * p["w_up"][l]
    pad_cols = lambda a: jnp.pad(a, ((0, 0), (0, ff_pad)))
    ffn_cw, ffn_cb = p["ffn_conv_w"][l].astype(F32), p["ffn_conv_b"][l].astype(F32).reshape(1, -1)
    return dict(
        xbc_tn=512 if l == 0 else 1024,
        w_z=cut(0, o_xbc), w_xbc=cut(o_xbc, o_dt), w_dt=cut(o_dt, o_hy), w_hy=cut(o_hy, o_sg),
        w_sg=cut(o_sg, o_gate), w_gate=cut(o_gate, o_gate + GATE_IN),
        b_gate=p["b_gate"][l].reshape(1, -1).astype(F32),
        ssm_conv_w=p["ssm_conv_w"][l], ssm_conv_b=p["ssm_conv_b"][l],
        dt_bias=p["ssm_dt_bias"][l].reshape(1, -1).astype(F32),
        a_log=p["ssm_a_log"][l].reshape(1, -1).astype(F32),
        e_fwd=(heads == chan_head).astype(BF16), e_bwd=(heads == chan_head + SSM_HEADS).astype(BF16),
        d_exp=jnp.repeat(p["ssm_d"][l].astype(F32), SSM_HEAD_DIM).reshape(1, -1),
        ssm_norm_g=p["ssm_norm_g"][l].reshape(1, -1).astype(F32),
        hy_conv_w=p["hy_conv_w"][l], hy_conv_b=p["hy_conv_b"][l],
        hy_mlp=tuple(p[k][l] for k in ("hy_w1", "hy_b1", "hy_w2", "hy_b2", "hy_w3", "hy_b3", "hy_freq", "hy_w4")),
        hy_bias=p["hy_bias"][l],
        sg_ln_g=p["sg_ln_g"][l], sg_ln_b=p["sg_ln_b"][l], sg_ws=p["sg_ws"][l], sg_bs=p["sg_bs"][l],
        w_br0=w_br[:SSM_WIDTH].astype(BF16), w_br1=w_br[SSM_WIDTH:SSM_WIDTH + HY_WIDTH].astype(BF16),
        w_br2=w_br[SSM_WIDTH + HY_WIDTH:].astype(BF16),
        w_out=p["w_out"][l].astype(BF16),
        w_up_g=pad_cols(w_up[:, :D_FF]).astype(BF16), w_up_v=pad_cols(w_up[:, D_FF:]).astype(BF16),
        ffn_cw_g=pad_cols(ffn_cw[:, :D_FF]), ffn_cw_v=pad_cols(ffn_cw[:, D_FF:]),
        ffn_cb_g=pad_cols(ffn_cb[:, :D_FF]), ffn_cb_v=pad_cols(ffn_cb[:, D_FF:]),
        w_down=jnp.pad(p["w_down"][l], ((0, ff_pad), (0, 0))).astype(BF16),
    )


def _ssd_branch(lw, xb, ssq, dt_raw, z_act, bt, seq, tm=1024):
    xbc_act = _mm_conv(xb, ssq, [(lw["w_xbc"], lw["ssm_conv_w"], lw["ssm_conv_b"], 0)], seq=seq,
                       width=SSM_XBC, combine=lambda c: (_silu(c),), n_out=1, tm=tm, tn=lw["xbc_tn"],
                       name="in_xbc_conv")[0].reshape(bt, seq, SSM_XBC)
    dt3 = dt_raw.reshape(bt, seq, 2 * SSM_HEADS)
    y_fwd = _ssd_call(xbc_act, dt3, lw["dt_bias"], lw["a_log"], lw["e_fwd"], reverse=False)
    y_m = _ssd_call(xbc_act, dt3, lw["dt_bias"], lw["a_log"], lw["e_bwd"], reverse=True, y_fwd=y_fwd,
                    z_act=z_act.reshape(bt, seq, SSM_WIDTH), d_exp=lw["d_exp"], norm_g=lw["ssm_norm_g"])
    return y_m.reshape(bt * seq, SSM_WIDTH)


def _hyena_branch(lw, xb, ssq, dft, bt, seq, tm=1024):
    co, sn, ct, st = dft
    groups = [(lw["w_hy"], lw["hy_conv_w"], lw["hy_conv_b"], off) for off in (0, HY_WIDTH, 2 * HY_WIDTH)]
    x0, w = _mm_conv(xb, ssq, groups, seq=seq, width=HY_WIDTH, combine=lambda c0, c1, c2: (c0, c2 * c1),
                     n_out=2, tm=tm, tn=256, name="in_hy_conv")
    x0 = x0.reshape(bt, seq, HY_WIDTH)
    w = w.reshape(bt, seq, HY_WIDTH)
    ks, kd = _hy_filters(seq, *lw["hy_mlp"])
    kre = _matmul(co, ks, out_dtype=F32, name="hy_kre", tm=512, tn=512)
    kim = _matmul(sn, kd, out_dtype=F32, name="hy_kim", tm=512, tn=512)
    yre, yim = _hy_fwd_call(co, sn, w, kre, kim)
    y_h = _hy_inv_call(ct, st, yre, yim, x0, w, lw["hy_bias"])
    return y_h.reshape(bt * seq, HY_WIDTH)


def _layer(x, xb, ssq, lw, dft, bt, seq, tm=1024):
    z_act = _matmul(xb, lw["w_z"], out_dtype=BF16, name="in_z", epilogue=_ep_silu, row_ssq=ssq)
    dt_raw = _matmul(xb, lw["w_dt"], out_dtype=F32, name="in_dt", row_ssq=ssq)
    sg_act = _matmul(xb, lw["w_sg"], out_dtype=BF16, name="in_sg", epilogue=_ep_gelu, row_ssq=ssq)
    gates = _matmul(xb, lw["w_gate"], out_dtype=BF16, name="in_gate", epilogue=_ep_gate, rows=(lw["b_gate"],),
                    row_ssq=ssq)
    y_m = _ssd_branch(lw, xb, ssq, dt_raw, z_act, bt, seq, tm)
    y_h = _hyena_branch(lw, xb, ssq, dft, bt, seq, tm)
    y_g = _sgu_call(sg_act.reshape(bt, seq, SG_IN), lw["sg_ln_g"], lw["sg_ln_b"], lw["sg_ws"],
                    lw["sg_bs"]).reshape(bt * seq, SG_WIDTH)
    merged = _merge_call(y_m, y_h, y_g, lw["w_br0"], lw["w_br1"], lw["w_br2"], gates)
    x, xb, ssq = _matmul(merged, lw["w_out"], out_dtype=F32, name="out_proj", epilogue=_ep_residual, tiles=(x,),
                         norm_out=True, tn=512)
    groups = [(lw["w_up_g"], lw["ffn_cw_g"], lw["ffn_cb_g"], 0), (lw["w_up_v"], lw["ffn_cw_v"], lw["ffn_cb_v"], 0)]
    act = _mm_conv(xb, ssq, groups, seq=seq, width=D_FF_PADDED, combine=lambda g, v: (_silu(g) * v,), n_out=1,
                   tm=tm, tn=512, name="ffn_up_conv")[0]
    return _matmul(act, lw["w_down"], out_dtype=F32, name="ffn_down", epilogue=_ep_residual, tiles=(x,),
                   norm_out=True, tm=512, tn=512)


def _trunk(x, layers, normf_g):
    bt, seq, d = x.shape
    dft = (*_dft_matrices(seq, inverse=False), *_dft_matrices(seq, inverse=True))
    xf = x.reshape(bt * seq, d)
    xb, ssq = _norm_stats(xf)
    for lw in layers:
        xf, xb, ssq = _layer(xf, xb, ssq, lw, dft, bt, seq)
    return _rmsnorm(xf, normf_g, F32).reshape(bt, seq, d)


def kernel(x_prompt, x_sample, norm1_g, w_in, b_gate, ssm_conv_w, ssm_conv_b, ssm_dt_bias, ssm_a_log, ssm_d,
           ssm_norm_g, hy_conv_w, hy_conv_b, hy_w1, hy_b1, hy_w2, hy_b2, hy_w3, hy_b3, hy_freq, hy_w4, hy_bias,
           sg_ln_g, sg_ln_b, sg_ws, sg_bs, w_br, w_out, norm2_g, w_up, ffn_conv_w, ffn_conv_b, w_down, normf_g):
    p = dict(norm1_g=norm1_g, w_in=w_in, b_gate=b_gate, ssm_conv_w=ssm_conv_w, ssm_conv_b=ssm_conv_b,
             ssm_dt_bias=ssm_dt_bias, ssm_a_log=ssm_a_log, ssm_d=ssm_d, ssm_norm_g=ssm_norm_g,
             hy_conv_w=hy_conv_w, hy_conv_b=hy_conv_b, hy_w1=hy_w1, hy_b1=hy_b1, hy_w2=hy_w2, hy_b2=hy_b2,
             hy_w3=hy_w3, hy_b3=hy_b3, hy_freq=hy_freq, hy_w4=hy_w4, hy_bias=hy_bias, sg_ln_g=sg_ln_g,
             sg_ln_b=sg_ln_b, sg_ws=sg_ws, sg_bs=sg_bs, w_br=w_br, w_out=w_out, norm2_g=norm2_g, w_up=w_up,
             ffn_conv_w=ffn_conv_w, ffn_conv_b=ffn_conv_b, w_down=w_down)
    layers = [_prep_layer(l, p) for l in range(w_in.shape[0])]
    return (_trunk(x_prompt, layers, normf_g), _trunk(x_sample, layers, normf_g))
```

```python
import functools
import math

import jax
import jax.numpy as jnp
from jax import lax
from jax.experimental import pallas as pl
from jax.experimental.pallas import tpu as pltpu

F32 = jnp.float32
BF16 = jnp.bfloat16

D_MODEL = 4096
SSM_WIDTH = D_MODEL
SSM_HEAD_DIM = 64
SSM_HEADS = SSM_WIDTH // SSM_HEAD_DIM
SSM_GROUPS = 8
SSM_STATE = 128
SSM_CONV = 5
SSD_CHUNK = 128
SSM_GN = SSM_GROUPS * SSM_STATE
SSM_XBC = SSM_WIDTH + 2 * SSM_GN
SSM_IN = SSM_WIDTH + SSM_XBC + 2 * SSM_HEADS
SSM_GROUP_WIDTH = SSM_WIDTH // SSM_GROUPS
HY_WIDTH = D_MODEL // 2
HY_SHORT = 3
HY_EMB = 33
HY_BANDS = (HY_EMB - 1) // 2
HY_ORDER = 64
HY_FAST_DECAY = 0.3
HY_SLOW_DECAY = 1.5
HY_TARGET = 1e-2
HY_IN = 3 * HY_WIDTH
SG_WIDTH = D_MODEL // 2
SG_CHUNK = 128
SG_GROUPS = 16
SG_GROUP_DIM = SG_WIDTH // SG_GROUPS
SG_IN = 2 * SG_WIDTH
N_BRANCH = 3
GATE_IN = N_BRANCH * D_MODEL
D_FF = 256 * math.ceil(8 * D_MODEL / 3 / 256)
FFN_CONV = 3
RMS_EPS = 1e-6
LN_EPS = 1e-5

V7X_SCOPED_VMEM_BYTES = 56 * 1024 * 1024
LANES = 128
SUBLANES_BF16 = 16
HALO_KEEP = 8
MM_CONV_PIECE_ROWS = 128
MM_PIECE_ROWS = 256


def _tile(dim, pref):
    t = pref
    while dim % t:
        t //= 2
    return t


def _params(semantics):
    return pltpu.CompilerParams(dimension_semantics=semantics, vmem_limit_bytes=V7X_SCOPED_VMEM_BYTES)


def _silu(x):
    return x * jax.nn.sigmoid(x)


def _rmsnorm_kernel(x_ref, g_ref, o_ref):
    x = x_ref[...]
    ms = jnp.mean(x * x, axis=-1, keepdims=True)
    o_ref[...] = (x * lax.rsqrt(ms + RMS_EPS) * g_ref[...]).astype(o_ref.dtype)


def _rmsnorm(x, g, out_dtype):
    m, d = x.shape
    tr = _tile(m, 256)
    return pl.pallas_call(
        _rmsnorm_kernel,
        grid=(m // tr,),
        in_specs=[pl.BlockSpec((tr, d), lambda i: (i, 0)), pl.BlockSpec((1, d), lambda i: (0, 0))],
        out_specs=pl.BlockSpec((tr, d), lambda i: (i, 0)),
        out_shape=jax.ShapeDtypeStruct((m, d), out_dtype),
        compiler_params=_params(("parallel",)),
        name="rmsnorm",
    )(x, g.reshape(1, d).astype(F32))


def _mm_kernel(a_ref, b_ref, *rest, n_rows, n_tiles, epilogue):
    row_refs = rest[:n_rows]
    tile_refs = rest[n_rows:n_rows + n_tiles]
    o_ref = rest[n_rows + n_tiles]
    tm = a_ref.shape[0]
    pm = min(tm, MM_PIECE_ROWS)
    for p in range(tm // pm):
        rows = slice(p * pm, (p + 1) * pm)
        acc = jnp.dot(a_ref[rows, :], b_ref[...], preferred_element_type=F32)
        if epilogue is not None:
            acc = epilogue(acc, *[r[...] for r in row_refs], *[t[rows, :] for t in tile_refs])
        o_ref[rows, :] = acc.astype(o_ref.dtype)


def _matmul(a, b, *, out_dtype, name, epilogue=None, rows=(), tiles=(), col0=0, width=None, tm=1024, tn=1024):
    m, k = a.shape
    n = b.shape[1] if width is None else width
    tm, tn = _tile(m, tm), _tile(n, tn)
    cb0 = col0 // tn
    in_specs = [pl.BlockSpec((tm, k), lambda i, j: (i, 0)), pl.BlockSpec((k, tn), lambda i, j: (0, cb0 + j))]
    in_specs += [pl.BlockSpec((1, tn), lambda i, j: (0, j)) for _ in rows]
    in_specs += [pl.BlockSpec((tm, tn), lambda i, j: (i, j)) for _ in tiles]
    return pl.pallas_call(
        functools.partial(_mm_kernel, n_rows=len(rows), n_tiles=len(tiles), epilogue=epilogue),
        grid=(m // tm, n // tn),
        in_specs=in_specs,
        out_specs=pl.BlockSpec((tm, tn), lambda i, j: (i, j)),
        out_shape=jax.ShapeDtypeStruct((m, n), out_dtype),
        compiler_params=_params(("parallel", "parallel")),
        name=name,
    )(a, b, *rows, *tiles)


def _ep_silu(acc):
    return _silu(acc)


def _ep_gelu(acc):
    return 0.5 * acc * (1.0 + lax.erf(acc * (1.0 / math.sqrt(2.0))))


def _ep_gate(acc, bias):
    return jax.nn.sigmoid(acc + bias)


def _ep_residual(acc, res):
    return acc + res


def _mm_conv_kernel(a_ref, *refs, n_groups, ksize, n_row_tiles, tiles_per_seq, piece_rows, combine):
    w_refs = refs[:n_groups]
    cw_refs = refs[n_groups:2 * n_groups]
    cb_refs = refs[2 * n_groups:3 * n_groups]
    out_refs = refs[3 * n_groups:-2]
    p_ref, carry_ref = refs[-2:]
    s = pl.program_id(0)
    tm = a_ref.shape[0]
    pad = ksize // 2

    @pl.when(s == 0)
    def _():
        p_ref[...] = jnp.zeros_like(p_ref)
        carry_ref[...] = jnp.zeros_like(carry_ref)

    prev_tile = jnp.maximum(s - 1, 0) % n_row_tiles
    has_prev = jnp.where(prev_tile % tiles_per_seq != 0, 1.0, 0.0)
    has_next = jnp.where(prev_tile % tiles_per_seq != tiles_per_seq - 1, 1.0, 0.0)
    pm = min(tm, piece_rows)
    n_pieces = tm // pm
    cur = [[None] * n_pieces for _ in range(n_groups)]
    for k in range(n_pieces):
        rows = slice(k * pm, (k + 1) * pm)
        a = a_ref[rows, :]
        for g in range(n_groups):
            cur[g][k] = jnp.dot(a, w_refs[g][...], preferred_element_type=F32)
        conv = []
        for g in range(n_groups):
            head = carry_ref[g] * has_prev if k == 0 else p_ref[g, k * pm - HALO_KEEP:k * pm, :]
            tail = (cur[g][0][:HALO_KEEP] * has_next if k == n_pieces - 1
                    else p_ref[g, (k + 1) * pm:(k + 1) * pm + HALO_KEEP, :])
            ext = jnp.concatenate([head, p_ref[g, rows, :], tail], axis=0)
            w = cw_refs[g][...]
            acc = cb_refs[g][...]
            for j in range(ksize):
                off = HALO_KEEP - pad + j
                acc = acc + w[j:j + 1, :] * ext[off:off + pm]
            conv.append(acc)
        for o_ref, val in zip(out_refs, combine(*conv)):
            o_ref[rows, :] = val.astype(o_ref.dtype)
    for g in range(n_groups):
        carry_ref[g] = p_ref[g, tm - HALO_KEEP:, :]
        for k in range(n_pieces):
            p_ref[g, k * pm:(k + 1) * pm, :] = cur[g][k]


def _mm_conv(a, groups, *, seq, width, combine, n_out, tm, tn, name, piece_rows=MM_CONV_PIECE_ROWS):
    t, k = a.shape
    ksize = groups[0][2].shape[0]
    tm, tn = _tile(seq, tm), _tile(width, tn)
    n_row_tiles, n_col_tiles = t // tm, width // tn
    steps = n_row_tiles * n_col_tiles
    n_groups = len(groups)
    cur_tile = lambda s: jnp.minimum(s, steps - 1)
    fin_tile = lambda s: jnp.maximum(s - 1, 0)
    w_offs = [g[1] // tn for g in groups]
    c_offs = [g[4] // tn for g in groups]
    in_specs = [pl.BlockSpec((tm, k), lambda s: (cur_tile(s) % n_row_tiles, 0))]
    in_specs += [pl.BlockSpec((k, tn), lambda s, o=o: (0, o + cur_tile(s) // n_row_tiles)) for o in w_offs]
    in_specs += [pl.BlockSpec((ksize, tn), lambda s, o=o: (0, o + fin_tile(s) // n_row_tiles)) for o in c_offs]
    in_specs += [pl.BlockSpec((1, tn), lambda s, o=o: (0, o + fin_tile(s) // n_row_tiles)) for o in c_offs]
    out_spec = pl.BlockSpec((tm, tn), lambda s: (fin_tile(s) % n_row_tiles, fin_tile(s) // n_row_tiles))
    return pl.pallas_call(
        functools.partial(_mm_conv_kernel, n_groups=n_groups, ksize=ksize, n_row_tiles=n_row_tiles,
                          tiles_per_seq=seq // tm, piece_rows=piece_rows, combine=combine),
        grid=(steps + 1,),
        in_specs=in_specs,
        out_specs=[out_spec] * n_out,
        out_shape=[jax.ShapeDtypeStruct((t, width), BF16)] * n_out,
        scratch_shapes=[pltpu.VMEM((n_groups, tm, tn), F32), pltpu.VMEM((n_groups, HALO_KEEP, tn), F32)],
        compiler_params=_params(("arbitrary",)),
        name=name,
    )(a, *[g[0] for g in groups], *[g[2].astype(F32) for g in groups],
      *[g[3].reshape(1, -1).astype(F32) for g in groups])


def _split3(x):
    hi = x.astype(BF16)
    r1 = x - hi.astype(F32)
    mid = r1.astype(BF16)
    lo = (r1 - mid.astype(F32)).astype(BF16)
    return hi, mid, lo


def _ssd_kernel(*refs, reverse, finalize):
    if finalize:
        (xs_ref, b_ref, c_ref, dtr_ref, dtb_ref, alog_ref, e_ref,
         yf_ref, z_ref, dexp_ref, ng_ref, o_ref, h_ref) = refs
    else:
        xs_ref, b_ref, c_ref, dtr_ref, dtb_ref, alog_ref, e_ref, o_ref, h_ref = refs
    t = SSD_CHUNK
    head0 = SSM_HEADS if reverse else 0

    @pl.when(pl.program_id(1) == 0)
    def _():
        h_ref[...] = jnp.zeros_like(h_ref)

    dt = jax.nn.softplus(dtr_ref[...] + dtb_ref[...])
    da = dt * (-jnp.exp(alog_ref[...]))
    row = lax.broadcasted_iota(jnp.int32, (t, t), 0)
    col = lax.broadcasted_iota(jnp.int32, (t, t), 1)
    if reverse:
        tri = (row > col).astype(BF16)
        tri_t = (col > row).astype(BF16)
        mask = col >= row
    else:
        tri = (row >= col).astype(BF16)
        tri_t = (col >= row).astype(BF16)
        mask = row >= col
    pieces = _split3(da)
    p = sum(jnp.dot(tri, x, preferred_element_type=F32) for x in pieces)
    p_t = sum(lax.dot_general(x, tri_t, (((0,), (0,)), ((), ())), preferred_element_type=F32)
              for x in pieces)
    total = jnp.sum(da, axis=0, keepdims=True)
    if reverse:
        dd = dt * jnp.exp(p)
        ea = jnp.exp(total - p)
    else:
        dd = dt * jnp.exp(total - p)
        ea = jnp.exp(p)
    q = jnp.concatenate([dt, dd, ea], axis=0).astype(BF16)
    etot = _split3(jnp.broadcast_to(jnp.exp(total), (SUBLANES_BF16, 2 * SSM_HEADS)))
    lane = lax.broadcasted_iota(jnp.int32, (t, 4 * SSM_HEAD_DIM), 1)

    for g in range(SSM_GROUPS):
        cols = slice(g * SSM_GROUP_WIDTH, (g + 1) * SSM_GROUP_WIDTH)
        scols = slice(g * SSM_STATE, (g + 1) * SSM_STATE)
        eg = e_ref[:, cols]
        ex = jnp.dot(q, eg, preferred_element_type=F32)
        dt_e, dd_e, ea_e = ex[0:t], ex[t:2 * t], ex[2 * t:3 * t]
        etot_e = sum(jnp.dot(x, eg, preferred_element_type=F32) for x in etot)[0:1]
        xs = xs_ref[:, cols].astype(F32)
        xc = (xs * dt_e).astype(BF16)
        xc2 = (xs * dd_e).astype(BF16)
        bg = b_ref[:, scols]
        cg = c_ref[:, scols]
        cb = lax.dot_general(cg, bg, (((1,), (1,)), ((), ())), preferred_element_type=F32)
        h_in = h_ref[:, cols]
        y = jnp.dot(cg, h_in.astype(BF16), preferred_element_type=F32) * ea_e
        s_new = lax.dot_general(bg, xc2, (((0,), (0,)), ((), ())), preferred_element_type=F32)
        h_ref[:, cols] = h_in * etot_e + s_new
        quads = []
        for qd in range(2):
            xq = xc[:, qd * 4 * SSM_HEAD_DIM:(qd + 1) * 4 * SSM_HEAD_DIM]
            yq = None
            for j in range(4):
                hc = head0 + g * 8 + qd * 4 + j
                pc = jnp.broadcast_to(p[:, hc:hc + 1], (t, t))
                pr = jnp.broadcast_to(p_t[hc:hc + 1, :], (t, t))
                seg = (pr - pc) if reverse else (pc - pr)
                dec = jnp.exp(jnp.where(mask, seg, -jnp.inf))
                m = (cb * dec).astype(BF16)
                in_head = (lane >= j * SSM_HEAD_DIM) & (lane < (j + 1) * SSM_HEAD_DIM)
                xm = jnp.where(in_head, xq, jnp.zeros_like(xq))
                d = jnp.dot(m, xm, preferred_element_type=F32)
                yq = d if yq is None else yq + d
            quads.append(yq)
        y = y + jnp.concatenate(quads, axis=1)
        if finalize:
            y = yf_ref[:, cols] + y + xs * dexp_ref[:, cols]
            y = y * z_ref[:, cols].astype(F32)
            ms = jnp.mean(y * y, axis=-1, keepdims=True)
            y = y * lax.rsqrt(ms + RMS_EPS) * ng_ref[:, cols]
        o_ref[:, cols] = y.astype(o_ref.dtype)


def _ssd_call(xbc_act, dt_raw, dt_bias, a_log, e_map, *, reverse, y_fwd=None, z_act=None, d_exp=None,
              norm_g=None):
    bt, seq, _ = xbc_act.shape
    t = SSD_CHUNK
    nc = seq // t
    finalize = y_fwd is not None
    cidx = (lambda c: nc - 1 - c) if reverse else (lambda c: c)
    gn_blocks = SSM_WIDTH // SSM_GN
    in_specs = [
        pl.BlockSpec((None, t, SSM_WIDTH), lambda b, c: (b, cidx(c), 0)),
        pl.BlockSpec((None, t, SSM_GN), lambda b, c: (b, cidx(c), gn_blocks)),
        pl.BlockSpec((None, t, SSM_GN), lambda b, c: (b, cidx(c), gn_blocks + 1)),
        pl.BlockSpec((None, t, 2 * SSM_HEADS), lambda b, c: (b, cidx(c), 0)),
        pl.BlockSpec((1, 2 * SSM_HEADS), lambda b, c: (0, 0)),
        pl.BlockSpec((1, 2 * SSM_HEADS), lambda b, c: (0, 0)),
        pl.BlockSpec((2 * SSM_HEADS, SSM_WIDTH), lambda b, c: (0, 0)),
    ]
    args = [xbc_act, xbc_act, xbc_act, dt_raw, dt_bias, a_log, e_map]
    if finalize:
        in_specs += [
            pl.BlockSpec((None, t, SSM_WIDTH), lambda b, c: (b, cidx(c), 0)),
            pl.BlockSpec((None, t, SSM_WIDTH), lambda b, c: (b, cidx(c), 0)),
            pl.BlockSpec((1, SSM_WIDTH), lambda b, c: (0, 0)),
            pl.BlockSpec((1, SSM_WIDTH), lambda b, c: (0, 0)),
        ]
        args += [y_fwd, z_act, d_exp, norm_g]
    return pl.pallas_call(
        functools.partial(_ssd_kernel, reverse=reverse, finalize=finalize),
        grid=(bt, nc),
        in_specs=in_specs,
        out_specs=pl.BlockSpec((None, t, SSM_WIDTH), lambda b, c: (b, cidx(c), 0)),
        out_shape=jax.ShapeDtypeStruct((bt, seq, SSM_WIDTH), BF16 if finalize else F32),
        scratch_shapes=[pltpu.VMEM((SSM_STATE, SSM_WIDTH), F32)],
        compiler_params=_params(("parallel", "arbitrary")),
        name="ssd_bwd" if reverse else "ssd_fwd",
    )(*args)


def _hy_filter_kernel(fr_ref, w1t_ref, w1c_ref, w1s_ref, b1_ref, w2_ref, b2_ref, w3_ref, b3_ref, fq_ref,
                      w4_ref, dl_ref, ks_ref, kd_ref, *, n):
    tl = ks_ref.shape[0]
    pos = (pl.program_id(0) * tl + lax.broadcasted_iota(jnp.int32, (tl, 1), 0)).astype(F32)
    tt = pos / float(n - 1)
    arg = fr_ref[...] * ((2.0 * math.pi / n) * pos)
    fq = fq_ref[...]
    dot = functools.partial(jnp.dot, preferred_element_type=F32)
    h = tt * w1t_ref[...] + dot(jnp.cos(arg), w1c_ref[...]) + dot(-jnp.sin(arg), w1s_ref[...]) + b1_ref[...]
    h = jnp.sin(fq * h)
    h = jnp.sin(fq * (dot(h, w2_ref[...]) + b2_ref[...]))
    h = jnp.sin(fq * (dot(h, w3_ref[...]) + b3_ref[...]))
    k = dot(h, w4_ref[...])
    win = jnp.exp(-tt * dl_ref[...])
    kf = k[:, :HY_WIDTH] * win
    kb = jnp.where(pos == 0.0, 0.0, k[:, HY_WIDTH:] * win)
    ks_ref[...] = (kf + kb).astype(ks_ref.dtype)
    kd_ref[...] = (kf - kb).astype(kd_ref.dtype)


def _hy_filters(n, w1, b1, w2, b2, w3, b3, freq, w4):
    tl = _tile(n, 256)
    fr = jnp.linspace(1e-4, HY_BANDS - 1, HY_BANDS, dtype=F32).reshape(1, HY_BANDS)
    min_decay = math.log(HY_TARGET) / HY_SLOW_DECAY
    max_decay = math.log(HY_TARGET) / HY_FAST_DECAY
    deltas = jnp.abs(jnp.linspace(min_decay, max_decay, HY_WIDTH, dtype=F32)).reshape(1, HY_WIDTH)
    w1 = w1.astype(F32)
    small = [fr, w1[0:1], w1[1:1 + HY_BANDS], w1[1 + HY_BANDS:], b1.reshape(1, -1), w2, b2.reshape(1, -1),
             w3, b3.reshape(1, -1), freq.reshape(1, -1), w4, deltas]
    small = [s.astype(F32) for s in small]
    out_spec = pl.BlockSpec((tl, HY_WIDTH), lambda i: (i, 0))
    return pl.pallas_call(
        functools.partial(_hy_filter_kernel, n=n),
        grid=(n // tl,),
        in_specs=[pl.BlockSpec(s.shape, lambda i: (0, 0)) for s in small],
        out_specs=[out_spec, out_spec],
        out_shape=[jax.ShapeDtypeStruct((n, HY_WIDTH), BF16)] * 2,
        compiler_params=_params(("parallel",)),
        name="hy_filter",
    )(*small)


def _dft_angles(m, n):
    m = jnp.bitwise_and(m, 4 * n - 1)
    m = jnp.where(m >= 2 * n, m - 4 * n, m)
    return m.astype(F32) * (math.pi / (2 * n))


def _dft_gen_kernel(c_ref, s_ref, cb_ref, sb_ref, *, n, inverse):
    tm, tn = c_ref.shape
    r0 = pl.program_id(1) * tm
    di = lax.broadcasted_iota(jnp.int32, (tm, tn), 0)
    c = pl.program_id(0) * tn + lax.broadcasted_iota(jnp.int32, (tm, tn), 1)
    c1 = pl.program_id(0) * tn + lax.broadcasted_iota(jnp.int32, (1, tn), 1)

    @pl.when(pl.program_id(1) == 0)
    def _():
        beta = _dft_angles((2 * c + 1) * di if inverse else 2 * di * c, n)
        cb_ref[...] = jnp.cos(beta)
        sb_ref[...] = jnp.sin(beta)

    alpha = _dft_angles((2 * c1 + 1) * r0 if inverse else (2 * r0 + 1) * c1, n)
    ca, sa = jnp.cos(alpha), jnp.sin(alpha)
    cb, sb = cb_ref[...], sb_ref[...]
    scale = (1.0 / n) if inverse else 1.0
    c_ref[...] = ((ca * cb - sa * sb) * scale).astype(c_ref.dtype)
    s_ref[...] = ((sa * cb + ca * sb) * (-scale)).astype(s_ref.dtype)


def _dft_matrices(n, inverse):
    tm, tn = _tile(n, 256), _tile(n, 512)
    spec = pl.BlockSpec((tm, tn), lambda j, i: (i, j))
    return pl.pallas_call(
        functools.partial(_dft_gen_kernel, n=n, inverse=inverse),
        grid=(n // tn, n // tm),
        out_specs=[spec, spec],
        out_shape=[jax.ShapeDtypeStruct((n, n), BF16)] * 2,
        scratch_shapes=[pltpu.VMEM((tm, tn), F32), pltpu.VMEM((tm, tn), F32)],
        compiler_params=_params(("parallel", "arbitrary")),
        name="dft_gen_inv" if inverse else "dft_gen_fwd",
    )()


def _hy_fwd_kernel(co_ref, sn_ref, w_ref, kre_ref, kim_ref, yre_ref, yim_ref):
    w = w_ref[...]
    wre = jnp.dot(co_ref[...], w, preferred_element_type=F32)
    wim = jnp.dot(sn_ref[...], w, preferred_element_type=F32)
    kre, kim = kre_ref[...], kim_ref[...]
    yre_ref[...] = (wre * kre - wim * kim).astype(yre_ref.dtype)
    yim_ref[...] = (wre * kim + wim * kre).astype(yim_ref.dtype)


def _hy_fwd_call(co, sn, w, kre, kim):
    bt, n, width = w.shape
    tm, tn = _tile(n, 512), _tile(width, 512)
    fspec = pl.BlockSpec((tm, n), lambda i, b, j: (i, 0))
    kspec = pl.BlockSpec((tm, tn), lambda i, b, j: (i, j))
    ospec = pl.BlockSpec((None, tm, tn), lambda i, b, j: (b, i, j))
    return pl.pallas_call(
        _hy_fwd_kernel,
        grid=(n // tm, bt, width // tn),
        in_specs=[fspec, fspec, pl.BlockSpec((None, n, tn), lambda i, b, j: (b, 0, j)), kspec, kspec],
        out_specs=[ospec, ospec],
        out_shape=[jax.ShapeDtypeStruct((bt, n, width), BF16)] * 2,
        compiler_params=_params(("parallel", "parallel", "parallel")),
        name="hy_dft_fwd",
    )(co, sn, w, kre, kim)


def _hy_inv_kernel(ct_ref, st_ref, yre_ref, yim_ref, x0_ref, w_ref, bias_ref, o_ref):
    yc = (jnp.dot(ct_ref[...], yre_ref[...], preferred_element_type=F32)
          + jnp.dot(st_ref[...], yim_ref[...], preferred_element_type=F32))
    o_ref[...] = (x0_ref[...].astype(F32) * (yc + w_ref[...].astype(F32) * bias_ref[...])).astype(o_ref.dtype)


def _hy_inv_call(ct, st, yre, yim, x0, w, bias):
    bt, n, width = w.shape
    tm, tn = _tile(n, 512), _tile(width, 512)
    fspec = pl.BlockSpec((tm, n), lambda i, b, j: (i, 0))
    yspec = pl.BlockSpec((None, n, tn), lambda i, b, j: (b, 0, j))
    tspec = pl.BlockSpec((None, tm, tn), lambda i, b, j: (b, i, j))
    return pl.pallas_call(
        _hy_inv_kernel,
        grid=(n // tm, bt, width // tn),
        in_specs=[fspec, fspec, yspec, yspec, tspec, tspec, pl.BlockSpec((1, tn), lambda i, b, j: (0, j))],
        out_specs=tspec,
        out_shape=jax.ShapeDtypeStruct((bt, n, width), BF16),
        compiler_params=_params(("parallel", "parallel", "parallel")),
        name="hy_dft_inv",
    )(ct, st, yre, yim, x0, w, bias.reshape(1, width).astype(F32))


def _sgu_kernel(x_ref, lng_ref, lnb_ref, ws_ref, bse_ref, o_ref):
    tl = x_ref.shape[0]
    v = x_ref[:, SG_WIDTH:].astype(F32)
    mu = jnp.mean(v, axis=-1, keepdims=True)
    vc = v - mu
    var = jnp.mean(vc * vc, axis=-1, keepdims=True)
    vn = (vc * lax.rsqrt(var + LN_EPS) * lng_ref[...] + lnb_ref[...]).astype(BF16)
    for g in range(SG_GROUPS):
        cols = slice(g * SG_GROUP_DIM, (g + 1) * SG_GROUP_DIM)
        wsg = ws_ref[g]
        for k in range(tl // SG_CHUNK):
            rows = slice(k * SG_CHUNK, (k + 1) * SG_CHUNK)
            mixed = jnp.dot(wsg, vn[rows, cols], preferred_element_type=F32) + bse_ref[:, cols]
            o_ref[rows, cols] = (x_ref[rows, cols].astype(F32) * mixed).astype(o_ref.dtype)


def _sgu_call(sg_act, ln_g, ln_b, ws, bs):
    bt, seq, _ = sg_act.shape
    tl = _tile(seq, 2 * SG_CHUNK)
    bs_exp = jnp.repeat(bs.astype(F32).T, SG_GROUP_DIM, axis=1)
    return pl.pallas_call(
        _sgu_kernel,
        grid=(bt, seq // tl),
        in_specs=[pl.BlockSpec((None, tl, SG_IN), lambda b, i: (b, i, 0)),
                  pl.BlockSpec((1, SG_WIDTH), lambda b, i: (0, 0)),
                  pl.BlockSpec((1, SG_WIDTH), lambda b, i: (0, 0)),
                  pl.BlockSpec((SG_GROUPS, SG_CHUNK, SG_CHUNK), lambda b, i: (0, 0, 0)),
                  pl.BlockSpec((SG_CHUNK, SG_WIDTH), lambda b, i: (0, 0))],
        out_specs=pl.BlockSpec((None, tl, SG_WIDTH), lambda b, i: (b, i, 0)),
        out_shape=jax.ShapeDtypeStruct((bt, seq, SG_WIDTH), BF16),
        compiler_params=_params(("parallel", "parallel")),
        name="sgu",
    )(sg_act, ln_g.reshape(1, -1).astype(F32), ln_b.reshape(1, -1).astype(F32), ws.astype(BF16), bs_exp)


def _merge_kernel(ym_ref, yh_ref, yg_ref, w0_ref, w1_ref, w2_ref, g0_ref, g1_ref, g2_ref, o_ref):
    dot = functools.partial(jnp.dot, preferred_element_type=F32)
    tm = o_ref.shape[0]
    pm = min(tm, MM_PIECE_ROWS)
    for p in range(tm // pm):
        rows = slice(p * pm, (p + 1) * pm)
        acc = g0_ref[rows, :].astype(F32) * dot(ym_ref[rows, :], w0_ref[...])
        acc = acc + g1_ref[rows, :].astype(F32) * dot(yh_ref[rows, :], w1_ref[...])
        acc = acc + g2_ref[rows, :].astype(F32) * dot(yg_ref[rows, :], w2_ref[...])
        o_ref[rows, :] = acc.astype(o_ref.dtype)


def _merge_call(y_m, y_h, y_g, w0, w1, w2, gates):
    m = y_m.shape[0]
    tm, tn = _tile(m, 512), 512
    nb = D_MODEL // tn
    aspec = lambda width: pl.BlockSpec((tm, width), lambda i, j: (i, 0))
    wspec = lambda width: pl.BlockSpec((width, tn), lambda i, j: (0, j))
    gspec = lambda br: pl.BlockSpec((tm, tn), lambda i, j: (i, br * nb + j))
    return pl.pallas_call(
        _merge_kernel,
        grid=(m // tm, nb),
        in_specs=[aspec(SSM_WIDTH), aspec(HY_WIDTH), aspec(SG_WIDTH), wspec(SSM_WIDTH), wspec(HY_WIDTH),
                  wspec(SG_WIDTH), gspec(0), gspec(1), gspec(2)],
        out_specs=pl.BlockSpec((tm, tn), lambda i, j: (i, j)),
        out_shape=jax.ShapeDtypeStruct((m, D_MODEL), BF16),
        compiler_params=_params(("parallel", "parallel")),
        name="merge",
    )(y_m, y_h, y_g, w0, w1, w2, gates, gates, gates)


def _prep_layer(l, p):
    w_in = p["w_in"][l]
    o_dt, o_hy = SSM_WIDTH + SSM_XBC, SSM_IN
    w_br = p["w_br"][l]
    heads = jnp.arange(2 * SSM_HEADS)[:, None]
    chan_head = (jnp.arange(SSM_WIDTH) // SSM_HEAD_DIM)[None, :]
    return dict(
        piece_rows=MM_CONV_PIECE_ROWS if l == 0 else 2 * MM_CONV_PIECE_ROWS,
        w_in=jnp.concatenate([w_in[:, :o_dt], w_in[:, o_hy:], w_in[:, o_dt:o_hy]], axis=1).astype(BF16),
        norm1_g=p["norm1_g"][l], norm2_g=p["norm2_g"][l],
        b_gate=p["b_gate"][l].reshape(1, -1).astype(F32),
        ssm_conv_w=p["ssm_conv_w"][l], ssm_conv_b=p["ssm_conv_b"][l],
        dt_bias=p["ssm_dt_bias"][l].reshape(1, -1).astype(F32),
        a_log=p["ssm_a_log"][l].reshape(1, -1).astype(F32),
        e_fwd=(heads == chan_head).astype(BF16), e_bwd=(heads == chan_head + SSM_HEADS).astype(BF16),
        d_exp=jnp.repeat(p["ssm_d"][l].astype(F32), SSM_HEAD_DIM).reshape(1, -1),
        ssm_norm_g=p["ssm_norm_g"][l].reshape(1, -1).astype(F32),
        hy_conv_w=p["hy_conv_w"][l], hy_conv_b=p["hy_conv_b"][l],
        hy_mlp=tuple(p[k][l] for k in ("hy_w1", "hy_b1", "hy_w2", "hy_b2", "hy_w3", "hy_b3", "hy_freq", "hy_w4")),
        hy_bias=p["hy_bias"][l],
        sg_ln_g=p["sg_ln_g"][l], sg_ln_b=p["sg_ln_b"][l], sg_ws=p["sg_ws"][l], sg_bs=p["sg_bs"][l],
        w_br0=w_br[:SSM_WIDTH].astype(BF16), w_br1=w_br[SSM_WIDTH:SSM_WIDTH + HY_WIDTH].astype(BF16),
        w_br2=w_br[SSM_WIDTH + HY_WIDTH:].astype(BF16),
        w_out=p["w_out"][l].astype(BF16), w_up=p["w_up"][l].astype(BF16),
        ffn_conv_w=p["ffn_conv_w"][l], ffn_conv_b=p["ffn_conv_b"][l], w_down=p["w_down"][l].astype(BF16),
    )


W_IN_COLS = dict(z=0, xbc=SSM_WIDTH, hy=SSM_WIDTH + SSM_XBC, sg=SSM_WIDTH + SSM_XBC + HY_IN,
                 gate=SSM_WIDTH + SSM_XBC + HY_IN + SG_IN, dt=SSM_WIDTH + SSM_XBC + HY_IN + SG_IN + GATE_IN)


def _ssd_branch(lw, h, dt_raw, z_act, bt, seq, tm=1024):
    xbc_act = _mm_conv(h, [(lw["w_in"], W_IN_COLS["xbc"], lw["ssm_conv_w"], lw["ssm_conv_b"], 0)], seq=seq,
                       width=SSM_XBC, combine=lambda c: (_silu(c),), n_out=1, tm=tm, tn=512,
                       piece_rows=lw["piece_rows"], name="in_xbc_conv")[0].reshape(bt, seq, SSM_XBC)
    dt3 = dt_raw.reshape(bt, seq, 2 * SSM_HEADS)
    y_fwd = _ssd_call(xbc_act, dt3, lw["dt_bias"], lw["a_log"], lw["e_fwd"], reverse=False)
    y_m = _ssd_call(xbc_act, dt3, lw["dt_bias"], lw["a_log"], lw["e_bwd"], reverse=True, y_fwd=y_fwd,
                    z_act=z_act.reshape(bt, seq, SSM_WIDTH), d_exp=lw["d_exp"], norm_g=lw["ssm_norm_g"])
    return y_m.reshape(bt * seq, SSM_WIDTH)


def _hyena_branch(lw, h, dft, bt, seq, tm=1024):
    co, sn, ct, st = dft
    groups = [(lw["w_in"], W_IN_COLS["hy"] + off, lw["hy_conv_w"], lw["hy_conv_b"], off)
              for off in (0, HY_WIDTH, 2 * HY_WIDTH)]
    x0, w = _mm_conv(h, groups, seq=seq, width=HY_WIDTH, combine=lambda c0, c1, c2: (c0, c2 * c1),
                     n_out=2, tm=tm, tn=256, piece_rows=lw["piece_rows"], name="in_hy_conv")
    x0 = x0.reshape(bt, seq, HY_WIDTH)
    w = w.reshape(bt, seq, HY_WIDTH)
    ks, kd = _hy_filters(seq, *lw["hy_mlp"])
    kre = _matmul(co, ks, out_dtype=F32, name="hy_kre", tm=512, tn=512)
    kim = _matmul(sn, kd, out_dtype=F32, name="hy_kim", tm=512, tn=512)
    yre, yim = _hy_fwd_call(co, sn, w, kre, kim)
    y_h = _hy_inv_call(ct, st, yre, yim, x0, w, lw["hy_bias"])
    return y_h.reshape(bt * seq, HY_WIDTH)


def _layer(x, lw, dft, bt, seq, tm=1024):
    h = _rmsnorm(x, lw["norm1_g"], BF16)
    w_in = lw["w_in"]
    z_act = _matmul(h, w_in, out_dtype=BF16, name="in_z", epilogue=_ep_silu, col0=W_IN_COLS["z"], width=SSM_WIDTH)
    dt_raw = _matmul(h, w_in, out_dtype=F32, name="in_dt", col0=W_IN_COLS["dt"], width=2 * SSM_HEADS)
    sg_act = _matmul(h, w_in, out_dtype=BF16, name="in_sg", epilogue=_ep_gelu, col0=W_IN_COLS["sg"], width=SG_IN)
    gates = _matmul(h, w_in, out_dtype=BF16, name="in_gate", epilogue=_ep_gate, rows=(lw["b_gate"],),
                    col0=W_IN_COLS["gate"], width=GATE_IN)
    y_m = _ssd_branch(lw, h, dt_raw, z_act, bt, seq, tm)
    y_h = _hyena_branch(lw, h, dft, bt, seq, tm)
    y_g = _sgu_call(sg_act.reshape(bt, seq, SG_IN), lw["sg_ln_g"], lw["sg_ln_b"], lw["sg_ws"],
                    lw["sg_bs"]).reshape(bt * seq, SG_WIDTH)
    merged = _merge_call(y_m, y_h, y_g, lw["w_br0"], lw["w_br1"], lw["w_br2"], gates)
    x = _matmul(merged, lw["w_out"], out_dtype=F32, name="out_proj", epilogue=_ep_residual, tiles=(x,))
    h2 = _rmsnorm(x, lw["norm2_g"], BF16)
    groups = [(lw["w_up"], off, lw["ffn_conv_w"], lw["ffn_conv_b"], off) for off in (0, D_FF)]
    act = _mm_conv(h2, groups, seq=seq, width=D_FF, combine=lambda g, v: (_silu(g) * v,), n_out=1,
                   tm=tm, tn=256, piece_rows=lw["piece_rows"], name="ffn_up_conv")[0]
    return _matmul(act, lw["w_down"], out_dtype=F32, name="ffn_down", epilogue=_ep_residual, tiles=(x,),
                   tm=512, tn=512)


def _trunk(x, layers, normf_g):
    bt, seq, d = x.shape
    dft = (*_dft_matrices(seq, inverse=False), *_dft_matrices(seq, inverse=True))
    xf = x.reshape(bt * seq, d)
    for lw in layers:
        xf = _layer(xf, lw, dft, bt, seq)
    return _rmsnorm(xf, normf_g, F32).reshape(bt, seq, d)


def kernel(x_prompt, x_sample, norm1_g, w_in, b_gate, ssm_conv_w, ssm_conv_b, ssm_dt_bias, ssm_a_log, ssm_d,
           ssm_norm_g, hy_conv_w, hy_conv_b, hy_w1, hy_b1, hy_w2, hy_b2, hy_w3, hy_b3, hy_freq, hy_w4, hy_bias,
           sg_ln_g, sg_ln_b, sg_ws, sg_bs, w_br, w_out, norm2_g, w_up, ffn_conv_w, ffn_conv_b, w_down, normf_g):
    p = dict(norm1_g=norm1_g, w_in=w_in, b_gate=b_gate, ssm_conv_w=ssm_conv_w, ssm_conv_b=ssm_conv_b,
             ssm_dt_bias=ssm_dt_bias, ssm_a_log=ssm_a_log, ssm_d=ssm_d, ssm_norm_g=ssm_norm_g,
             hy_conv_w=hy_conv_w, hy_conv_b=hy_conv_b, hy_w1=hy_w1, hy_b1=hy_b1, hy_w2=hy_w2, hy_b2=hy_b2,
             hy_w3=hy_w3, hy_b3=hy_b3, hy_freq=hy_freq, hy_w4=hy_w4, hy_bias=hy_bias, sg_ln_g=sg_ln_g,
             sg_ln_b=sg_ln_b, sg_ws=sg_ws, sg_bs=sg_bs, w_br=w_br, w_out=w_out, norm2_g=norm2_g, w_up=w_up,
             ffn_conv_w=ffn_conv_w, ffn_conv_b=ffn_conv_b, w_down=w_down)
    layers = [_prep_layer(l, p) for l in range(w_in.shape[0])]
    return (_trunk(x_prompt, layers, normf_g), _trunk(x_sample, layers, normf_g))
```

```python
import functools
import math
from typing import NamedTuple

import jax
import jax.numpy as jnp
from jax import lax
from jax.experimental import pallas as pl
from jax.experimental.pallas import tpu as pltpu

F32 = jnp.float32
BF16 = jnp.bfloat16

D_MODEL = 4096
SSM_WIDTH = D_MODEL
SSM_HEAD_DIM = 64
SSM_HEADS = SSM_WIDTH // SSM_HEAD_DIM
SSM_GROUPS = 8
SSM_STATE = 128
SSM_CONV = 5
SSD_CHUNK = 128
SSM_GN = SSM_GROUPS * SSM_STATE
SSM_XBC = SSM_WIDTH + 2 * SSM_GN
SSM_IN = SSM_WIDTH + SSM_XBC + 2 * SSM_HEADS
SSM_GROUP_WIDTH = SSM_WIDTH // SSM_GROUPS
HY_WIDTH = D_MODEL // 2
HY_SHORT = 3
HY_EMB = 33
HY_BANDS = (HY_EMB - 1) // 2
HY_ORDER = 64
HY_FAST_DECAY = 0.3
HY_SLOW_DECAY = 1.5
HY_TARGET = 1e-2
HY_IN = 3 * HY_WIDTH
SG_WIDTH = D_MODEL // 2
SG_CHUNK = 128
SG_GROUPS = 16
SG_GROUP_DIM = SG_WIDTH // SG_GROUPS
SG_IN = 2 * SG_WIDTH
N_BRANCH = 3
GATE_IN = N_BRANCH * D_MODEL
D_FF = 256 * math.ceil(8 * D_MODEL / 3 / 256)
FFN_CONV = 3
RMS_EPS = 1e-6
LN_EPS = 1e-5

V7X_SCOPED_VMEM_BYTES = 56 * 1024 * 1024
LANES = 128
SUBLANES_BF16 = 16
HALO_KEEP = 8
MM_CONV_PIECE_ROWS = 128
MM_PIECE_ROWS = 256


def _tile(dim, pref):
    t = pref
    while dim % t:
        t //= 2
    return t


def _params(semantics):
    return pltpu.CompilerParams(dimension_semantics=semantics, vmem_limit_bytes=V7X_SCOPED_VMEM_BYTES)


def _silu(x):
    return x * jax.nn.sigmoid(x)


class WSlice(NamedTuple):
    arr: jax.Array
    layer: int
    row0: int
    rows: int
    col0: int


def _wspec(w, tn, col_index):
    return pl.BlockSpec((pl.Element(1), pl.Element(w.rows), pl.Element(tn)),
                        lambda *ids: (w.layer, w.row0, pl.multiple_of(w.col0 + tn * col_index(*ids), LANES)))


def _rmsnorm_kernel(x_ref, g_ref, o_ref):
    x = x_ref[...]
    ms = jnp.mean(x * x, axis=-1, keepdims=True)
    o_ref[...] = (x * lax.rsqrt(ms + RMS_EPS) * g_ref[...]).astype(o_ref.dtype)


def _rmsnorm(x, g, out_dtype):
    m, d = x.shape
    tr = _tile(m, 256)
    return pl.pallas_call(
        _rmsnorm_kernel,
        grid=(m // tr,),
        in_specs=[pl.BlockSpec((tr, d), lambda i: (i, 0)), pl.BlockSpec((1, d), lambda i: (0, 0))],
        out_specs=pl.BlockSpec((tr, d), lambda i: (i, 0)),
        out_shape=jax.ShapeDtypeStruct((m, d), out_dtype),
        compiler_params=_params(("parallel",)),
        name="rmsnorm",
    )(x, g.reshape(1, d).astype(F32))


def _mm_kernel(a_ref, b_ref, *rest, n_rows, n_tiles, epilogue):
    row_refs = rest[:n_rows]
    tile_refs = rest[n_rows:n_rows + n_tiles]
    o_ref = rest[n_rows + n_tiles]
    tm = a_ref.shape[0]
    pm = min(tm, MM_PIECE_ROWS)
    for p in range(tm // pm):
        rows = slice(p * pm, (p + 1) * pm)
        acc = jnp.dot(a_ref[rows, :], b_ref[0], preferred_element_type=F32)
        if epilogue is not None:
            acc = epilogue(acc, *[r[...] for r in row_refs], *[t[rows, :] for t in tile_refs])
        o_ref[rows, :] = acc.astype(o_ref.dtype)


def _matmul(a, w, n, *, out_dtype, name, epilogue=None, rows=(), tiles=(), tm=1024, tn=1024):
    m, k = a.shape
    tm, tn = _tile(m, tm), _tile(n, tn)
    in_specs = [pl.BlockSpec((tm, k), lambda i, j: (i, 0)), _wspec(w, tn, lambda i, j: j)]
    in_specs += [pl.BlockSpec((1, tn), lambda i, j: (0, j)) for _ in rows]
    in_specs += [pl.BlockSpec((tm, tn), lambda i, j: (i, j)) for _ in tiles]
    return pl.pallas_call(
        functools.partial(_mm_kernel, n_rows=len(rows), n_tiles=len(tiles), epilogue=epilogue),
        grid=(m // tm, n // tn),
        in_specs=in_specs,
        out_specs=pl.BlockSpec((tm, tn), lambda i, j: (i, j)),
        out_shape=jax.ShapeDtypeStruct((m, n), out_dtype),
        compiler_params=_params(("parallel", "parallel")),
        name=name,
    )(a, w.arr, *rows, *tiles)


def _ep_silu(acc):
    return _silu(acc)


def _ep_gelu(acc):
    return 0.5 * acc * (1.0 + lax.erf(acc * (1.0 / math.sqrt(2.0))))


def _ep_gate(acc, bias):
    return jax.nn.sigmoid(acc + bias)


def _ep_residual(acc, res):
    return acc + res


def _mm_conv_kernel(a_ref, *refs, n_groups, ksize, n_row_tiles, tiles_per_seq, piece_rows, combine):
    w_refs = refs[:n_groups]
    cw_refs = refs[n_groups:2 * n_groups]
    cb_refs = refs[2 * n_groups:3 * n_groups]
    out_refs = refs[3 * n_groups:-2]
    p_ref, carry_ref = refs[-2:]
    s = pl.program_id(0)
    tm = a_ref.shape[0]
    pad = ksize // 2

    @pl.when(s == 0)
    def _():
        p_ref[...] = jnp.zeros_like(p_ref)
        carry_ref[...] = jnp.zeros_like(carry_ref)

    prev_tile = jnp.maximum(s - 1, 0) % n_row_tiles
    has_prev = jnp.where(prev_tile % tiles_per_seq != 0, 1.0, 0.0)
    has_next = jnp.where(prev_tile % tiles_per_seq != tiles_per_seq - 1, 1.0, 0.0)
    pm = min(tm, piece_rows)
    n_pieces = tm // pm
    cur = [[None] * n_pieces for _ in range(n_groups)]
    for k in range(n_pieces):
        rows = slice(k * pm, (k + 1) * pm)
        a = a_ref[rows, :]
        for g in range(n_groups):
            cur[g][k] = jnp.dot(a, w_refs[g][0], preferred_element_type=F32)
        conv = []
        for g in range(n_groups):
            head = carry_ref[g] * has_prev if k == 0 else p_ref[g, k * pm - HALO_KEEP:k * pm, :]
            tail = (cur[g][0][:HALO_KEEP] * has_next if k == n_pieces - 1
                    else p_ref[g, (k + 1) * pm:(k + 1) * pm + HALO_KEEP, :])
            ext = jnp.concatenate([head, p_ref[g, rows, :], tail], axis=0)
            w = cw_refs[g][...]
            acc = cb_refs[g][...]
            for j in range(ksize):
                off = HALO_KEEP - pad + j
                acc = acc + w[j:j + 1, :] * ext[off:off + pm]
            conv.append(acc)
        for o_ref, val in zip(out_refs, combine(*conv)):
            o_ref[rows, :] = val.astype(o_ref.dtype)
    for g in range(n_groups):
        carry_ref[g] = p_ref[g, tm - HALO_KEEP:, :]
        for k in range(n_pieces):
            p_ref[g, k * pm:(k + 1) * pm, :] = cur[g][k]


def _mm_conv(a, groups, *, seq, width, combine, n_out, tm, tn, name, piece_rows=MM_CONV_PIECE_ROWS):
    t, k = a.shape
    ksize = groups[0][1].shape[0]
    tm, tn = _tile(seq, tm), _tile(width, tn)
    n_row_tiles, n_col_tiles = t // tm, width // tn
    steps = n_row_tiles * n_col_tiles
    n_groups = len(groups)
    cur_tile = lambda s: jnp.minimum(s, steps - 1)
    fin_tile = lambda s: jnp.maximum(s - 1, 0)
    c_offs = [g[3] // tn for g in groups]
    in_specs = [pl.BlockSpec((tm, k), lambda s: (cur_tile(s) % n_row_tiles, 0))]
    in_specs += [_wspec(g[0], tn, lambda s: cur_tile(s) // n_row_tiles) for g in groups]
    in_specs += [pl.BlockSpec((ksize, tn), lambda s, o=o: (0, o + fin_tile(s) // n_row_tiles)) for o in c_offs]
    in_specs += [pl.BlockSpec((1, tn), lambda s, o=o: (0, o + fin_tile(s) // n_row_tiles)) for o in c_offs]
    out_spec = pl.BlockSpec((tm, tn), lambda s: (fin_tile(s) % n_row_tiles, fin_tile(s) // n_row_tiles))
    return pl.pallas_call(
        functools.partial(_mm_conv_kernel, n_groups=n_groups, ksize=ksize, n_row_tiles=n_row_tiles,
                          tiles_per_seq=seq // tm, piece_rows=piece_rows, combine=combine),
        grid=(steps + 1,),
        in_specs=in_specs,
        out_specs=[out_spec] * n_out,
        out_shape=[jax.ShapeDtypeStruct((t, width), BF16)] * n_out,
        scratch_shapes=[pltpu.VMEM((n_groups, tm, tn), F32), pltpu.VMEM((n_groups, HALO_KEEP, tn), F32)],
        compiler_params=_params(("arbitrary",)),
        name=name,
    )(a, *[g[0].arr for g in groups], *[g[1].astype(F32) for g in groups],
      *[g[2].reshape(1, -1).astype(F32) for g in groups])


def _split3(x):
    hi = x.astype(BF16)
    r1 = x - hi.astype(F32)
    mid = r1.astype(BF16)
    lo = (r1 - mid.astype(F32)).astype(BF16)
    return hi, mid, lo


def _ssd_kernel(*refs, reverse, finalize):
    if finalize:
        (xs_ref, b_ref, c_ref, dtr_ref, dtb_ref, alog_ref, e_ref,
         yf_ref, z_ref, dexp_ref, ng_ref, o_ref, h_ref) = refs
    else:
        xs_ref, b_ref, c_ref, dtr_ref, dtb_ref, alog_ref, e_ref, o_ref, h_ref = refs
    t = SSD_CHUNK
    head0 = SSM_HEADS if reverse else 0

    @pl.when(pl.program_id(1) == 0)
    def _():
        h_ref[...] = jnp.zeros_like(h_ref)

    dt = jax.nn.softplus(dtr_ref[...] + dtb_ref[...])
    da = dt * (-jnp.exp(alog_ref[...]))
    row = lax.broadcasted_iota(jnp.int32, (t, t), 0)
    col = lax.broadcasted_iota(jnp.int32, (t, t), 1)
    if reverse:
        tri = (row > col).astype(BF16)
        tri_t = (col > row).astype(BF16)
        mask = col >= row
    else:
        tri = (row >= col).astype(BF16)
        tri_t = (col >= row).astype(BF16)
        mask = row >= col
    pieces = _split3(da)
    p = sum(jnp.dot(tri, x, preferred_element_type=F32) for x in pieces)
    p_t = sum(lax.dot_general(x, tri_t, (((0,), (0,)), ((), ())), preferred_element_type=F32)
              for x in pieces)
    total = jnp.sum(da, axis=0, keepdims=True)
    if reverse:
        dd = dt * jnp.exp(p)
        ea = jnp.exp(total - p)
    else:
        dd = dt * jnp.exp(total - p)
        ea = jnp.exp(p)
    q = jnp.concatenate([dt, dd, ea], axis=0).astype(BF16)
    etot = _split3(jnp.broadcast_to(jnp.exp(total), (SUBLANES_BF16, 2 * SSM_HEADS)))
    lane = lax.broadcasted_iota(jnp.int32, (t, 4 * SSM_HEAD_DIM), 1)

    for g in range(SSM_GROUPS):
        cols = slice(g * SSM_GROUP_WIDTH, (g + 1) * SSM_GROUP_WIDTH)
        scols = slice(g * SSM_STATE, (g + 1) * SSM_STATE)
        eg = e_ref[:, cols]
        ex = jnp.dot(q, eg, preferred_element_type=F32)
        dt_e, dd_e, ea_e = ex[0:t], ex[t:2 * t], ex[2 * t:3 * t]
        etot_e = sum(jnp.dot(x, eg, preferred_element_type=F32) for x in etot)[0:1]
        xs = xs_ref[:, cols].astype(F32)
        xc = (xs * dt_e).astype(BF16)
        xc2 = (xs * dd_e).astype(BF16)
        bg = b_ref[:, scols]
        cg = c_ref[:, scols]
        cb = lax.dot_general(cg, bg, (((1,), (1,)), ((), ())), preferred_element_type=F32)
        h_in = h_ref[:, cols]
        y = jnp.dot(cg, h_in.astype(BF16), preferred_element_type=F32) * ea_e
        s_new = lax.dot_general(bg, xc2, (((0,), (0,)), ((), ())), preferred_element_type=F32)
        h_ref[:, cols] = h_in * etot_e + s_new
        quads = []
        for qd in range(2):
            xq = xc[:, qd * 4 * SSM_HEAD_DIM:(qd + 1) * 4 * SSM_HEAD_DIM]
            yq = None
            for j in range(4):
                hc = head0 + g * 8 + qd * 4 + j
                pc = jnp.broadcast_to(p[:, hc:hc + 1], (t, t))
                pr = jnp.broadcast_to(p_t[hc:hc + 1, :], (t, t))
                seg = (pr - pc) if reverse else (pc - pr)
                dec = jnp.exp(jnp.where(mask, seg, -jnp.inf))
                m = (cb * dec).astype(BF16)
                in_head = (lane >= j * SSM_HEAD_DIM) & (lane < (j + 1) * SSM_HEAD_DIM)
                xm = jnp.where(in_head, xq, jnp.zeros_like(xq))
                d = jnp.dot(m, xm, preferred_element_type=F32)
                yq = d if yq is None else yq + d
            quads.append(yq)
        y = y + jnp.concatenate(quads, axis=1)
        if finalize:
            y = yf_ref[:, cols] + y + xs * dexp_ref[:, cols]
            y = y * z_ref[:, cols].astype(F32)
            ms = jnp.mean(y * y, axis=-1, keepdims=True)
            y = y * lax.rsqrt(ms + RMS_EPS) * ng_ref[:, cols]
        o_ref[:, cols] = y.astype(o_ref.dtype)


def _ssd_call(xbc_act, dt_raw, dt_bias, a_log, e_map, *, reverse, y_fwd=None, z_act=None, d_exp=None,
              norm_g=None):
    bt, seq, _ = xbc_act.shape
    t = SSD_CHUNK
    nc = seq // t
    finalize = y_fwd is not None
    cidx = (lambda c: nc - 1 - c) if reverse else (lambda c: c)
    gn_blocks = SSM_WIDTH // SSM_GN
    in_specs = [
        pl.BlockSpec((None, t, SSM_WIDTH), lambda b, c: (b, cidx(c), 0)),
        pl.BlockSpec((None, t, SSM_GN), lambda b, c: (b, cidx(c), gn_blocks)),
        pl.BlockSpec((None, t, SSM_GN), lambda b, c: (b, cidx(c), gn_blocks + 1)),
        pl.BlockSpec((None, t, 2 * SSM_HEADS), lambda b, c: (b, cidx(c), 0)),
        pl.BlockSpec((1, 2 * SSM_HEADS), lambda b, c: (0, 0)),
        pl.BlockSpec((1, 2 * SSM_HEADS), lambda b, c: (0, 0)),
        pl.BlockSpec((2 * SSM_HEADS, SSM_WIDTH), lambda b, c: (0, 0)),
    ]
    args = [xbc_act, xbc_act, xbc_act, dt_raw, dt_bias, a_log, e_map]
    if finalize:
        in_specs += [
            pl.BlockSpec((None, t, SSM_WIDTH), lambda b, c: (b, cidx(c), 0)),
            pl.BlockSpec((None, t, SSM_WIDTH), lambda b, c: (b, cidx(c), 0)),
            pl.BlockSpec((1, SSM_WIDTH), lambda b, c: (0, 0)),
            pl.BlockSpec((1, SSM_WIDTH), lambda b, c: (0, 0)),
        ]
        args += [y_fwd, z_act, d_exp, norm_g]
    return pl.pallas_call(
        functools.partial(_ssd_kernel, reverse=reverse, finalize=finalize),
        grid=(bt, nc),
        in_specs=in_specs,
        out_specs=pl.BlockSpec((None, t, SSM_WIDTH), lambda b, c: (b, cidx(c), 0)),
        out_shape=jax.ShapeDtypeStruct((bt, seq, SSM_WIDTH), BF16 if finalize else F32),
        scratch_shapes=[pltpu.VMEM((SSM_STATE, SSM_WIDTH), F32)],
        compiler_params=_params(("parallel", "arbitrary")),
        name="ssd_bwd" if reverse else "ssd_fwd",
    )(*args)


def _hy_filter_kernel(fr_ref, w1t_ref, w1c_ref, w1s_ref, b1_ref, w2_ref, b2_ref, w3_ref, b3_ref, fq_ref,
                      w4_ref, dl_ref, ks_ref, kd_ref, *, n):
    tl = ks_ref.shape[0]
    pos = (pl.program_id(0) * tl + lax.broadcasted_iota(jnp.int32, (tl, 1), 0)).astype(F32)
    tt = pos / float(n - 1)
    arg = fr_ref[...] * ((2.0 * math.pi / n) * pos)
    fq = fq_ref[...]
    dot = functools.partial(jnp.dot, preferred_element_type=F32)
    h = tt * w1t_ref[...] + dot(jnp.cos(arg), w1c_ref[...]) + dot(-jnp.sin(arg), w1s_ref[...]) + b1_ref[...]
    h = jnp.sin(fq * h)
    h = jnp.sin(fq * (dot(h, w2_ref[...]) + b2_ref[...]))
    h = jnp.sin(fq * (dot(h, w3_ref[...]) + b3_ref[...]))
    k = dot(h, w4_ref[...])
    win = jnp.exp(-tt * dl_ref[...])
    kf = k[:, :HY_WIDTH] * win
    kb = jnp.where(pos == 0.0, 0.0, k[:, HY_WIDTH:] * win)
    ks_ref[...] = (kf + kb).astype(ks_ref.dtype)
    kd_ref[...] = (kf - kb).astype(kd_ref.dtype)


def _hy_filters(n, w1, b1, w2, b2, w3, b3, freq, w4):
    tl = _tile(n, 256)
    fr = jnp.linspace(1e-4, HY_BANDS - 1, HY_BANDS, dtype=F32).reshape(1, HY_BANDS)
    min_decay = math.log(HY_TARGET) / HY_SLOW_DECAY
    max_decay = math.log(HY_TARGET) / HY_FAST_DECAY
    deltas = jnp.abs(jnp.linspace(min_decay, max_decay, HY_WIDTH, dtype=F32)).reshape(1, HY_WIDTH)
    w1 = w1.astype(F32)
    small = [fr, w1[0:1], w1[1:1 + HY_BANDS], w1[1 + HY_BANDS:], b1.reshape(1, -1), w2, b2.reshape(1, -1),
             w3, b3.reshape(1, -1), freq.reshape(1, -1), w4, deltas]
    small = [s.astype(F32) for s in small]
    out_spec = pl.BlockSpec((tl, HY_WIDTH), lambda i: (i, 0))
    return pl.pallas_call(
        functools.partial(_hy_filter_kernel, n=n),
        grid=(n // tl,),
        in_specs=[pl.BlockSpec(s.shape, lambda i: (0, 0)) for s in small],
        out_specs=[out_spec, out_spec],
        out_shape=[jax.ShapeDtypeStruct((n, HY_WIDTH), BF16)] * 2,
        compiler_params=_params(("parallel",)),
        name="hy_filter",
    )(*small)


def _dft_angles(m, n):
    m = jnp.bitwise_and(m, 4 * n - 1)
    m = jnp.where(m >= 2 * n, m - 4 * n, m)
    return m.astype(F32) * (math.pi / (2 * n))


def _dft_gen_kernel(c_ref, s_ref, cb_ref, sb_ref, *, n, inverse):
    tm, tn = c_ref.shape
    r0 = pl.program_id(1) * tm
    di = lax.broadcasted_iota(jnp.int32, (tm, tn), 0)
    c = pl.program_id(0) * tn + lax.broadcasted_iota(jnp.int32, (tm, tn), 1)
    c1 = pl.program_id(0) * tn + lax.broadcasted_iota(jnp.int32, (1, tn), 1)

    @pl.when(pl.program_id(1) == 0)
    def _():
        beta = _dft_angles((2 * c + 1) * di if inverse else 2 * di * c, n)
        cb_ref[...] = jnp.cos(beta)
        sb_ref[...] = jnp.sin(beta)

    alpha = _dft_angles((2 * c1 + 1) * r0 if inverse else (2 * r0 + 1) * c1, n)
    ca, sa = jnp.cos(alpha), jnp.sin(alpha)
    cb, sb = cb_ref[...], sb_ref[...]
    scale = (1.0 / n) if inverse else 1.0
    c_ref[...] = ((ca * cb - sa * sb) * scale).astype(c_ref.dtype)
    s_ref[...] = ((sa * cb + ca * sb) * (-scale)).astype(s_ref.dtype)


def _dft_matrices(n, inverse):
    tm, tn = _tile(n, 256), _tile(n, 512)
    spec = pl.BlockSpec((tm, tn), lambda j, i: (i, j))
    return pl.pallas_call(
        functools.partial(_dft_gen_kernel, n=n, inverse=inverse),
        grid=(n // tn, n // tm),
        out_specs=[spec, spec],
        out_shape=[jax.ShapeDtypeStruct((n, n), BF16)] * 2,
        scratch_shapes=[pltpu.VMEM((tm, tn), F32), pltpu.VMEM((tm, tn), F32)],
        compiler_params=_params(("parallel", "arbitrary")),
        name="dft_gen_inv" if inverse else "dft_gen_fwd",
    )()


def _hy_fwd_kernel(co_ref, sn_ref, w_ref, kre_ref, kim_ref, yre_ref, yim_ref):
    w = w_ref[...]
    wre = jnp.dot(co_ref[...], w, preferred_element_type=F32)
    wim = jnp.dot(sn_ref[...], w, preferred_element_type=F32)
    kre, kim = kre_ref[...], kim_ref[...]
    yre_ref[...] = (wre * kre - wim * kim).astype(yre_ref.dtype)
    yim_ref[...] = (wre * kim + wim * kre).astype(yim_ref.dtype)


def _hy_fwd_call(co, sn, w, kre, kim):
    bt, n, width = w.shape
    tm, tn = _tile(n, 512), _tile(width, 512)
    fspec = pl.BlockSpec((tm, n), lambda i, b, j: (i, 0))
    kspec = pl.BlockSpec((tm, tn), lambda i, b, j: (i, j))
    ospec = pl.BlockSpec((None, tm, tn), lambda i, b, j: (b, i, j))
    return pl.pallas_call(
        _hy_fwd_kernel,
        grid=(n // tm, bt, width // tn),
        in_specs=[fspec, fspec, pl.BlockSpec((None, n, tn), lambda i, b, j: (b, 0, j)), kspec, kspec],
        out_specs=[ospec, ospec],
        out_shape=[jax.ShapeDtypeStruct((bt, n, width), BF16)] * 2,
        compiler_params=_params(("parallel", "parallel", "parallel")),
        name="hy_dft_fwd",
    )(co, sn, w, kre, kim)


def _hy_inv_kernel(ct_ref, st_ref, yre_ref, yim_ref, x0_ref, w_ref, bias_ref, o_ref):
    yc = (jnp.dot(ct_ref[...], yre_ref[...], preferred_element_type=F32)
          + jnp.dot(st_ref[...], yim_ref[...], preferred_element_type=F32))
    o_ref[...] = (x0_ref[...].astype(F32) * (yc + w_ref[...].astype(F32) * bias_ref[...])).astype(o_ref.dtype)


def _hy_inv_call(ct, st, yre, yim, x0, w, bias):
    bt, n, width = w.shape
    tm, tn = _tile(n, 512), _tile(width, 512)
    fspec = pl.BlockSpec((tm, n), lambda i, b, j: (i, 0))
    yspec = pl.BlockSpec((None, n, tn), lambda i, b, j: (b, 0, j))
    tspec = pl.BlockSpec((None, tm, tn), lambda i, b, j: (b, i, j))
    return pl.pallas_call(
        _hy_inv_kernel,
        grid=(n // tm, bt, width // tn),
        in_specs=[fspec, fspec, yspec, yspec, tspec, tspec, pl.BlockSpec((1, tn), lambda i, b, j: (0, j))],
        out_specs=tspec,
        out_shape=jax.ShapeDtypeStruct((bt, n, width), BF16),
        compiler_params=_params(("parallel", "parallel", "parallel")),
        name="hy_dft_inv",
    )(ct, st, yre, yim, x0, w, bias.reshape(1, width).astype(F32))


def _sgu_kernel(x_ref, lng_ref, lnb_ref, ws_ref, bse_ref, o_ref):
    tl = x_ref.shape[0]
    v = x_ref[:, SG_WIDTH:].astype(F32)
    mu = jnp.mean(v, axis=-1, keepdims=True)
    vc = v - mu
    var = jnp.mean(vc * vc, axis=-1, keepdims=True)
    vn = (vc * lax.rsqrt(var + LN_EPS) * lng_ref[...] + lnb_ref[...]).astype(BF16)
    for g in range(SG_GROUPS):
        cols = slice(g * SG_GROUP_DIM, (g + 1) * SG_GROUP_DIM)
        wsg = ws_ref[g]
        for k in range(tl // SG_CHUNK):
            rows = slice(k * SG_CHUNK, (k + 1) * SG_CHUNK)
            mixed = jnp.dot(wsg, vn[rows, cols], preferred_element_type=F32) + bse_ref[:, cols]
            o_ref[rows, cols] = (x_ref[rows, cols].astype(F32) * mixed).astype(o_ref.dtype)


def _sgu_call(sg_act, ln_g, ln_b, ws, bs):
    bt, seq, _ = sg_act.shape
    tl = _tile(seq, 2 * SG_CHUNK)
    bs_exp = jnp.repeat(bs.astype(F32).T, SG_GROUP_DIM, axis=1)
    return pl.pallas_call(
        _sgu_kernel,
        grid=(bt, seq // tl),
        in_specs=[pl.BlockSpec((None, tl, SG_IN), lambda b, i: (b, i, 0)),
                  pl.BlockSpec((1, SG_WIDTH), lambda b, i: (0, 0)),
                  pl.BlockSpec((1, SG_WIDTH), lambda b, i: (0, 0)),
                  pl.BlockSpec((SG_GROUPS, SG_CHUNK, SG_CHUNK), lambda b, i: (0, 0, 0)),
                  pl.BlockSpec((SG_CHUNK, SG_WIDTH), lambda b, i: (0, 0))],
        out_specs=pl.BlockSpec((None, tl, SG_WIDTH), lambda b, i: (b, i, 0)),
        out_shape=jax.ShapeDtypeStruct((bt, seq, SG_WIDTH), BF16),
        compiler_params=_params(("parallel", "parallel")),
        name="sgu",
    )(sg_act, ln_g.reshape(1, -1).astype(F32), ln_b.reshape(1, -1).astype(F32), ws.astype(BF16), bs_exp)


def _merge_kernel(ym_ref, yh_ref, yg_ref, w0_ref, w1_ref, w2_ref, g0_ref, g1_ref, g2_ref, o_ref):
    dot = functools.partial(jnp.dot, preferred_element_type=F32)
    tm = o_ref.shape[0]
    pm = min(tm, MM_PIECE_ROWS)
    for p in range(tm // pm):
        rows = slice(p * pm, (p + 1) * pm)
        acc = g0_ref[rows, :].astype(F32) * dot(ym_ref[rows, :], w0_ref[0])
        acc = acc + g1_ref[rows, :].astype(F32) * dot(yh_ref[rows, :], w1_ref[0])
        acc = acc + g2_ref[rows, :].astype(F32) * dot(yg_ref[rows, :], w2_ref[0])
        o_ref[rows, :] = acc.astype(o_ref.dtype)


def _merge_call(y_m, y_h, y_g, w_br, layer, gates):
    m = y_m.shape[0]
    tm, tn = _tile(m, 512), 512
    nb = D_MODEL // tn
    aspec = lambda width: pl.BlockSpec((tm, width), lambda i, j: (i, 0))
    wspec = lambda row0, rows: _wspec(WSlice(w_br, layer, row0, rows, 0), tn, lambda i, j: j)
    gspec = lambda br: pl.BlockSpec((tm, tn), lambda i, j: (i, br * nb + j))
    return pl.pallas_call(
        _merge_kernel,
        grid=(m // tm, nb),
        in_specs=[aspec(SSM_WIDTH), aspec(HY_WIDTH), aspec(SG_WIDTH), wspec(0, SSM_WIDTH),
                  wspec(SSM_WIDTH, HY_WIDTH), wspec(SSM_WIDTH + HY_WIDTH, SG_WIDTH), gspec(0), gspec(1), gspec(2)],
        out_specs=pl.BlockSpec((tm, tn), lambda i, j: (i, j)),
        out_shape=jax.ShapeDtypeStruct((m, D_MODEL), BF16),
        compiler_params=_params(("parallel", "parallel")),
        name="merge",
    )(y_m, y_h, y_g, w_br, w_br, w_br, gates, gates, gates)


def _prep_layer(l, p):
    heads = jnp.arange(2 * SSM_HEADS)[:, None]
    chan_head = (jnp.arange(SSM_WIDTH) // SSM_HEAD_DIM)[None, :]
    return dict(
        layer=l, conv_piece_rows=256 if l == 0 else 512,
        w_in=p["w_in"], w_br=p["w_br"], w_out=p["w_out"], w_up=p["w_up"], w_down=p["w_down"],
        norm1_g=p["norm1_g"][l], norm2_g=p["norm2_g"][l],
        b_gate=p["b_gate"][l].reshape(1, -1).astype(F32),
        ssm_conv_w=p["ssm_conv_w"][l], ssm_conv_b=p["ssm_conv_b"][l],
        dt_bias=p["ssm_dt_bias"][l].reshape(1, -1).astype(F32),
        a_log=p["ssm_a_log"][l].reshape(1, -1).astype(F32),
        e_fwd=(heads == chan_head).astype(BF16), e_bwd=(heads == chan_head + SSM_HEADS).astype(BF16),
        d_exp=jnp.repeat(p["ssm_d"][l].astype(F32), SSM_HEAD_DIM).reshape(1, -1),
        ssm_norm_g=p["ssm_norm_g"][l].reshape(1, -1).astype(F32),
        hy_conv_w=p["hy_conv_w"][l], hy_conv_b=p["hy_conv_b"][l],
        hy_mlp=tuple(p[k][l] for k in ("hy_w1", "hy_b1", "hy_w2", "hy_b2", "hy_w3", "hy_b3", "hy_freq", "hy_w4")),
        hy_bias=p["hy_bias"][l],
        sg_ln_g=p["sg_ln_g"][l], sg_ln_b=p["sg_ln_b"][l], sg_ws=p["sg_ws"][l], sg_bs=p["sg_bs"][l],
        ffn_conv_w=p["ffn_conv_w"][l], ffn_conv_b=p["ffn_conv_b"][l],
    )


W_IN_COLS = dict(z=0, xbc=SSM_WIDTH, dt=SSM_WIDTH + SSM_XBC, hy=SSM_IN, sg=SSM_IN + HY_IN,
                 gate=SSM_IN + HY_IN + SG_IN)


def _w_in(lw, group, extra=0):
    return WSlice(lw["w_in"], lw["layer"], 0, D_MODEL, W_IN_COLS[group] + extra)


def _ssd_branch(lw, h, dt_raw, z_act, bt, seq, tm=1024):
    xbc_act = _mm_conv(h, [(_w_in(lw, "xbc"), lw["ssm_conv_w"], lw["ssm_conv_b"], 0)], seq=seq,
                       width=SSM_XBC, combine=lambda c: (_silu(c),), n_out=1, tm=tm, tn=512,
                       piece_rows=lw["conv_piece_rows"], name="in_xbc_conv")[0].reshape(bt, seq, SSM_XBC)
    dt3 = dt_raw.reshape(bt, seq, 2 * SSM_HEADS)
    y_fwd = _ssd_call(xbc_act, dt3, lw["dt_bias"], lw["a_log"], lw["e_fwd"], reverse=False)
    y_m = _ssd_call(xbc_act, dt3, lw["dt_bias"], lw["a_log"], lw["e_bwd"], reverse=True, y_fwd=y_fwd,
                    z_act=z_act.reshape(bt, seq, SSM_WIDTH), d_exp=lw["d_exp"], norm_g=lw["ssm_norm_g"])
    return y_m.reshape(bt * seq, SSM_WIDTH)


def _hyena_branch(lw, h, dft, bt, seq, tm=1024):
    co, sn, ct, st = dft
    groups = [(_w_in(lw, "hy", off), lw["hy_conv_w"], lw["hy_conv_b"], off) for off in (0, HY_WIDTH, 2 * HY_WIDTH)]
    x0, w = _mm_conv(h, groups, seq=seq, width=HY_WIDTH, combine=lambda c0, c1, c2: (c0, c2 * c1),
                     n_out=2, tm=tm, tn=256, name="in_hy_conv")
    x0 = x0.reshape(bt, seq, HY_WIDTH)
    w = w.reshape(bt, seq, HY_WIDTH)
    ks, kd = _hy_filters(seq, *lw["hy_mlp"])
    kre = _matmul(co, WSlice(ks[None], 0, 0, seq, 0), HY_WIDTH, out_dtype=F32, name="hy_kre", tm=512, tn=512)
    kim = _matmul(sn, WSlice(kd[None], 0, 0, seq, 0), HY_WIDTH, out_dtype=F32, name="hy_kim", tm=512, tn=512)
    yre, yim = _hy_fwd_call(co, sn, w, kre, kim)
    y_h = _hy_inv_call(ct, st, yre, yim, x0, w, lw["hy_bias"])
    return y_h.reshape(bt * seq, HY_WIDTH)


def _layer(x, lw, dft, bt, seq, tm=1024):
    layer = lw["layer"]
    h = _rmsnorm(x, lw["norm1_g"], BF16)
    z_act = _matmul(h, _w_in(lw, "z"), SSM_WIDTH, out_dtype=BF16, name="in_z", epilogue=_ep_silu)
    dt_raw = _matmul(h, _w_in(lw, "dt"), 2 * SSM_HEADS, out_dtype=F32, name="in_dt")
    sg_act = _matmul(h, _w_in(lw, "sg"), SG_IN, out_dtype=BF16, name="in_sg", epilogue=_ep_gelu)
    gates = _matmul(h, _w_in(lw, "gate"), GATE_IN, out_dtype=BF16, name="in_gate", epilogue=_ep_gate,
                    rows=(lw["b_gate"],))
    y_m = _ssd_branch(lw, h, dt_raw, z_act, bt, seq, tm)
    y_h = _hyena_branch(lw, h, dft, bt, seq, tm)
    y_g = _sgu_call(sg_act.reshape(bt, seq, SG_IN), lw["sg_ln_g"], lw["sg_ln_b"], lw["sg_ws"],
                    lw["sg_bs"]).reshape(bt * seq, SG_WIDTH)
    merged = _merge_call(y_m, y_h, y_g, lw["w_br"], layer, gates)
    x = _matmul(merged, WSlice(lw["w_out"], layer, 0, D_MODEL, 0), D_MODEL, out_dtype=F32, name="out_proj",
                epilogue=_ep_residual, tiles=(x,))
    h2 = _rmsnorm(x, lw["norm2_g"], BF16)
    groups = [(WSlice(lw["w_up"], layer, 0, D_MODEL, off), lw["ffn_conv_w"], lw["ffn_conv_b"], off)
              for off in (0, D_FF)]
    act = _mm_conv(h2, groups, seq=seq, width=D_FF, combine=lambda g, v: (_silu(g) * v,), n_out=1,
                   tm=tm, tn=256, piece_rows=lw["conv_piece_rows"], name="ffn_up_conv")[0]
    return _matmul(act, WSlice(lw["w_down"], layer, 0, D_FF, 0), D_MODEL, out_dtype=F32, name="ffn_down",
                   epilogue=_ep_residual, tiles=(x,), tm=512, tn=512)


def _trunk(x, layers, normf_g):
    bt, seq, d = x.shape
    dft = (*_dft_matrices(seq, inverse=False), *_dft_matrices(seq, inverse=True))
    xf = x.reshape(bt * seq, d)
    for lw in layers:
        xf = _layer(xf, lw, dft, bt, seq)
    return _rmsnorm(xf, normf_g, F32).reshape(bt, seq, d)


def kernel(x_prompt, x_sample, norm1_g, w_in, b_gate, ssm_conv_w, ssm_conv_b, ssm_dt_bias, ssm_a_log, ssm_d,
           ssm_norm_g, hy_conv_w, hy_conv_b, hy_w1, hy_b1, hy_w2, hy_b2, hy_w3, hy_b3, hy_freq, hy_w4, hy_bias,
           sg_ln_g, sg_ln_b, sg_ws, sg_bs, w_br, w_out, norm2_g, w_up, ffn_conv_w, ffn_conv_b, w_down, normf_g):
    bf = lambda w: w.astype(BF16)
    p = dict(norm1_g=norm1_g, w_in=bf(w_in), b_gate=b_gate, ssm_conv_w=ssm_conv_w, ssm_conv_b=ssm_conv_b,
             ssm_dt_bias=ssm_dt_bias, ssm_a_log=ssm_a_log, ssm_d=ssm_d, ssm_norm_g=ssm_norm_g,
             hy_conv_w=hy_conv_w, hy_conv_b=hy_conv_b, hy_w1=hy_w1, hy_b1=hy_b1, hy_w2=hy_w2, hy_b2=hy_b2,
             hy_w3=hy_w3, hy_b3=hy_b3, hy_freq=hy_freq, hy_w4=hy_w4, hy_bias=hy_bias, sg_ln_g=sg_ln_g,
             sg_ln_b=sg_ln_b, sg_ws=sg_ws, sg_bs=sg_bs, w_br=bf(w_br), w_out=bf(w_out), norm2_g=norm2_g,
             w_up=bf(w_up), ffn_conv_w=ffn_conv_w, ffn_conv_b=ffn_conv_b, w_down=bf(w_down))
    layers = [_prep_layer(l, p) for l in range(w_in.shape[0])]
    return (_trunk(x_prompt, layers, normf_g), _trunk(x_sample, layers, normf_g))
```

```python
import functools
import math
from typing import NamedTuple

import jax
import jax.numpy as jnp
from jax import lax
from jax.experimental import pallas as pl
from jax.experimental.pallas import tpu as pltpu

F32 = jnp.float32
BF16 = jnp.bfloat16

D_MODEL = 4096
SSM_WIDTH = D_MODEL
SSM_HEAD_DIM = 64
SSM_HEADS = SSM_WIDTH // SSM_HEAD_DIM
SSM_GROUPS = 8
SSM_STATE = 128
SSM_CONV = 5
SSD_CHUNK = 128
SSM_GN = SSM_GROUPS * SSM_STATE
SSM_XBC = SSM_WIDTH + 2 * SSM_GN
SSM_IN = SSM_WIDTH + SSM_XBC + 2 * SSM_HEADS
SSM_GROUP_WIDTH = SSM_WIDTH // SSM_GROUPS
HY_WIDTH = D_MODEL // 2
HY_SHORT = 3
HY_EMB = 33
HY_BANDS = (HY_EMB - 1) // 2
HY_ORDER = 64
HY_FAST_DECAY = 0.3
HY_SLOW_DECAY = 1.5
HY_TARGET = 1e-2
HY_IN = 3 * HY_WIDTH
SG_WIDTH = D_MODEL // 2
SG_CHUNK = 128
SG_GROUPS = 16
SG_GROUP_DIM = SG_WIDTH // SG_GROUPS
SG_IN = 2 * SG_WIDTH
N_BRANCH = 3
GATE_IN = N_BRANCH * D_MODEL
D_FF = 256 * math.ceil(8 * D_MODEL / 3 / 256)
FFN_CONV = 3
RMS_EPS = 1e-6
LOG2_E = 1.0 / math.log(2.0)
LN_EPS = 1e-5

V7X_SCOPED_VMEM_BYTES = 56 * 1024 * 1024
LANES = 128
SUBLANES_BF16 = 16
HALO_KEEP = 8
MM_CONV_PIECE_ROWS = 128
MM_PIECE_ROWS = 256


def _tile(dim, pref):
    t = pref
    while dim % t:
        t //= 2
    return t


def _params(semantics):
    return pltpu.CompilerParams(dimension_semantics=semantics, vmem_limit_bytes=V7X_SCOPED_VMEM_BYTES)


def _silu(x):
    return x * jax.nn.sigmoid(x)


class WSlice(NamedTuple):
    arr: jax.Array
    layer: int
    row0: int
    rows: int
    col0: int


def _wspec(w, tn, col_index):
    return pl.BlockSpec((pl.Element(1), pl.Element(w.rows), pl.Element(tn)),
                        lambda *ids: (w.layer, w.row0, pl.multiple_of(w.col0 + tn * col_index(*ids), LANES)))


def _rmsnorm_kernel(x_ref, g_ref, o_ref):
    x = x_ref[...]
    ms = jnp.mean(x * x, axis=-1, keepdims=True)
    o_ref[...] = (x * lax.rsqrt(ms + RMS_EPS) * g_ref[...]).astype(o_ref.dtype)


def _rmsnorm(x, g, out_dtype):
    m, d = x.shape
    tr = _tile(m, 256)
    return pl.pallas_call(
        _rmsnorm_kernel,
        grid=(m // tr,),
        in_specs=[pl.BlockSpec((tr, d), lambda i: (i, 0)), pl.BlockSpec((1, d), lambda i: (0, 0))],
        out_specs=pl.BlockSpec((tr, d), lambda i: (i, 0)),
        out_shape=jax.ShapeDtypeStruct((m, d), out_dtype),
        compiler_params=_params(("parallel",)),
        name="rmsnorm",
    )(x, g.reshape(1, d).astype(F32))


def _mm_kernel(a_ref, b_ref, *rest, n_rows, n_tiles, epilogue):
    row_refs = rest[:n_rows]
    tile_refs = rest[n_rows:n_rows + n_tiles]
    o_ref = rest[n_rows + n_tiles]
    tm = a_ref.shape[0]
    pm = min(tm, MM_PIECE_ROWS)
    for p in range(tm // pm):
        rows = slice(p * pm, (p + 1) * pm)
        acc = jnp.dot(a_ref[rows, :], b_ref[0], preferred_element_type=F32)
        if epilogue is not None:
            acc = epilogue(acc, *[r[...] for r in row_refs], *[t[rows, :] for t in tile_refs])
        o_ref[rows, :] = acc.astype(o_ref.dtype)


def _matmul(a, w, n, *, out_dtype, name, epilogue=None, rows=(), tiles=(), tm=1024, tn=1024):
    m, k = a.shape
    tm, tn = _tile(m, tm), _tile(n, tn)
    in_specs = [pl.BlockSpec((tm, k), lambda i, j: (i, 0)), _wspec(w, tn, lambda i, j: j)]
    in_specs += [pl.BlockSpec((1, tn), lambda i, j: (0, j)) for _ in rows]
    in_specs += [pl.BlockSpec((tm, tn), lambda i, j: (i, j)) for _ in tiles]
    return pl.pallas_call(
        functools.partial(_mm_kernel, n_rows=len(rows), n_tiles=len(tiles), epilogue=epilogue),
        grid=(m // tm, n // tn),
        in_specs=in_specs,
        out_specs=pl.BlockSpec((tm, tn), lambda i, j: (i, j)),
        out_shape=jax.ShapeDtypeStruct((m, n), out_dtype),
        compiler_params=_params(("parallel", "parallel")),
        name=name,
    )(a, w.arr, *rows, *tiles)


def _ep_silu(acc):
    return _silu(acc)


def _ep_gelu(acc):
    return 0.5 * acc * (1.0 + lax.erf(acc * (1.0 / math.sqrt(2.0))))


def _ep_gate(acc, bias):
    return jax.nn.sigmoid(acc + bias)


def _ep_residual(acc, res):
    return acc + res


def _mm_conv_kernel(a_ref, *refs, n_groups, ksize, n_row_tiles, tiles_per_seq, piece_rows, combine):
    w_refs = refs[:n_groups]
    cw_refs = refs[n_groups:2 * n_groups]
    cb_refs = refs[2 * n_groups:3 * n_groups]
    out_refs = refs[3 * n_groups:-2]
    p_ref, carry_ref = refs[-2:]
    s = pl.program_id(0)
    tm = a_ref.shape[0]
    pad = ksize // 2

    @pl.when(s == 0)
    def _():
        p_ref[...] = jnp.zeros_like(p_ref)
        carry_ref[...] = jnp.zeros_like(carry_ref)

    prev_tile = jnp.maximum(s - 1, 0) % n_row_tiles
    has_prev = jnp.where(prev_tile % tiles_per_seq != 0, 1.0, 0.0)
    has_next = jnp.where(prev_tile % tiles_per_seq != tiles_per_seq - 1, 1.0, 0.0)
    pm = min(tm, piece_rows)
    n_pieces = tm // pm
    cur = [[None] * n_pieces for _ in range(n_groups)]
    for k in range(n_pieces):
        rows = slice(k * pm, (k + 1) * pm)
        a = a_ref[rows, :]
        for g in range(n_groups):
            cur[g][k] = jnp.dot(a, w_refs[g][0], preferred_element_type=F32)
        conv = []
        for g in range(n_groups):
            head = carry_ref[g] * has_prev if k == 0 else p_ref[g, k * pm - HALO_KEEP:k * pm, :]
            tail = (cur[g][0][:HALO_KEEP] * has_next if k == n_pieces - 1
                    else p_ref[g, (k + 1) * pm:(k + 1) * pm + HALO_KEEP, :])
            ext = jnp.concatenate([head, p_ref[g, rows, :], tail], axis=0)
            w = cw_refs[g][...]
            acc = cb_refs[g][...]
            for j in range(ksize):
                off = HALO_KEEP - pad + j
                acc = acc + w[j:j + 1, :] * ext[off:off + pm]
            conv.append(acc)
        for o_ref, val in zip(out_refs, combine(*conv)):
            o_ref[rows, :] = val.astype(o_ref.dtype)
    for g in range(n_groups):
        carry_ref[g] = p_ref[g, tm - HALO_KEEP:, :]
        for k in range(n_pieces):
            p_ref[g, k * pm:(k + 1) * pm, :] = cur[g][k]


def _mm_conv(a, groups, *, seq, width, combine, n_out, tm, tn, name, piece_rows=MM_CONV_PIECE_ROWS):
    t, k = a.shape
    ksize = groups[0][1].shape[0]
    tm, tn = _tile(seq, tm), _tile(width, tn)
    n_row_tiles, n_col_tiles = t // tm, width // tn
    steps = n_row_tiles * n_col_tiles
    n_groups = len(groups)
    cur_tile = lambda s: jnp.minimum(s, steps - 1)
    fin_tile = lambda s: jnp.maximum(s - 1, 0)
    c_offs = [g[3] // tn for g in groups]
    in_specs = [pl.BlockSpec((tm, k), lambda s: (cur_tile(s) % n_row_tiles, 0))]
    in_specs += [_wspec(g[0], tn, lambda s: cur_tile(s) // n_row_tiles) for g in groups]
    in_specs += [pl.BlockSpec((ksize, tn), lambda s, o=o: (0, o + fin_tile(s) // n_row_tiles)) for o in c_offs]
    in_specs += [pl.BlockSpec((1, tn), lambda s, o=o: (0, o + fin_tile(s) // n_row_tiles)) for o in c_offs]
    out_spec = pl.BlockSpec((tm, tn), lambda s: (fin_tile(s) % n_row_tiles, fin_tile(s) // n_row_tiles))
    return pl.pallas_call(
        functools.partial(_mm_conv_kernel, n_groups=n_groups, ksize=ksize, n_row_tiles=n_row_tiles,
                          tiles_per_seq=seq // tm, piece_rows=piece_rows, combine=combine),
        grid=(steps + 1,),
        in_specs=in_specs,
        out_specs=[out_spec] * n_out,
        out_shape=[jax.ShapeDtypeStruct((t, width), BF16)] * n_out,
        scratch_shapes=[pltpu.VMEM((n_groups, tm, tn), F32), pltpu.VMEM((n_groups, HALO_KEEP, tn), F32)],
        compiler_params=_params(("arbitrary",)),
        name=name,
    )(a, *[g[0].arr for g in groups], *[g[1].astype(F32) for g in groups],
      *[g[2].reshape(1, -1).astype(F32) for g in groups])


def _split3(x):
    hi = x.astype(BF16)
    r1 = x - hi.astype(F32)
    mid = r1.astype(BF16)
    lo = (r1 - mid.astype(F32)).astype(BF16)
    return hi, mid, lo


def _ssd_kernel(*refs, second):
    if second:
        (xs_ref, b_ref, c_ref, dtr_ref, dtb_ref, alog_ref, e_ref,
         yf_ref, z_ref, dexp_ref, ng_ref, o_ref, h_ref) = refs
    else:
        xs_ref, b_ref, c_ref, dtr_ref, dtb_ref, alog_ref, e_ref, o_ref, h_ref = refs
    t = SSD_CHUNK

    @pl.when(pl.program_id(1) == 0)
    def _():
        h_ref[...] = jnp.zeros_like(h_ref)

    dt = jax.nn.softplus(dtr_ref[...] + dtb_ref[...])
    da = dt * (-jnp.exp(alog_ref[...]))
    row = lax.broadcasted_iota(jnp.int32, (t, t), 0)
    col = lax.broadcasted_iota(jnp.int32, (t, t), 1)
    tri = (row >= col).astype(BF16)
    pieces = _split3(da)
    p_inc = sum(jnp.dot(tri, x, preferred_element_type=F32) for x in pieces)
    total = jnp.sum(da, axis=0, keepdims=True)
    if second:
        p = p_inc - da
        dd = dt * jnp.exp(p)
        ea = jnp.exp(total - p)
    else:
        dd = dt * jnp.exp(total - p_inc)
        ea = jnp.exp(p_inc)
    q = jnp.concatenate([dd, ea], axis=0).astype(BF16)
    etot = _split3(jnp.broadcast_to(jnp.exp(total), (SUBLANES_BF16, 2 * SSM_HEADS)))
    if second:
        tri_t = (col >= row).astype(BF16)
        p_inc_t = sum(lax.dot_general(x, tri_t, (((0,), (0,)), ((), ())), preferred_element_type=F32)
                      for x in pieces)
        dt_t = dt.T
        log2_dt_t = jnp.log2(dt_t)
        col_f = p_inc * LOG2_E
        row_f = p_inc_t * LOG2_E - log2_dt_t
        col_b = p * LOG2_E
        row_b = (p_inc_t - da.T) * LOG2_E + log2_dt_t
        lower = row >= col
        diag = row == col
        lane = lax.broadcasted_iota(jnp.int32, (t, 4 * SSM_HEAD_DIM), 1)

    for g in range(SSM_GROUPS):
        cols = slice(g * SSM_GROUP_WIDTH, (g + 1) * SSM_GROUP_WIDTH)
        scols = slice(g * SSM_STATE, (g + 1) * SSM_STATE)
        eg = e_ref[:, cols]
        ex = jnp.dot(q, eg, preferred_element_type=F32)
        dd_e, ea_e = ex[0:t], ex[t:2 * t]
        etot_e = sum(jnp.dot(x, eg, preferred_element_type=F32) for x in etot)[0:1]
        xs_b = xs_ref[:, cols]
        xs = xs_b.astype(F32)
        xc2 = (xs * dd_e).astype(BF16)
        bg = b_ref[:, scols]
        cg = c_ref[:, scols]
        h_in = h_ref[:, cols]
        y = jnp.dot(cg, h_in.astype(BF16), preferred_element_type=F32) * ea_e
        s_new = lax.dot_general(bg, xc2, (((0,), (0,)), ((), ())), preferred_element_type=F32)
        h_ref[:, cols] = h_in * etot_e + s_new
        if second:
            cb = lax.dot_general(cg, bg, (((1,), (1,)), ((), ())), preferred_element_type=F32)
            quads = []
            for qd in range(2):
                xq = xs_b[:, qd * 4 * SSM_HEAD_DIM:(qd + 1) * 4 * SSM_HEAD_DIM]
                mats, xms = [], []
                for j in range(4):
                    hf = g * 8 + qd * 4 + j
                    hb = SSM_HEADS + hf
                    bc = lambda v: jnp.broadcast_to(v, (t, t))
                    seg = jnp.where(lower, bc(col_f[:, hf:hf + 1]) - bc(row_f[hf:hf + 1, :]),
                                    bc(row_b[hb:hb + 1, :]) - bc(col_b[:, hb:hb + 1]))
                    w = jnp.exp2(seg) + jnp.where(diag, bc(dt_t[hb:hb + 1, :]), 0.0)
                    mats.append((cb * w).astype(BF16))
                    in_head = (lane >= j * SSM_HEAD_DIM) & (lane < (j + 1) * SSM_HEAD_DIM)
                    xms.append(jnp.where(in_head, xq, jnp.zeros_like(xq)))
                quads.append(jnp.dot(jnp.concatenate(mats, axis=1), jnp.concatenate(xms, axis=0),
                                     preferred_element_type=F32))
            y = y + jnp.concatenate(quads, axis=1)
            y = yf_ref[:, cols] + y + xs * dexp_ref[:, cols]
            y = y * z_ref[:, cols].astype(F32)
            ms = jnp.mean(y * y, axis=-1, keepdims=True)
            y = y * lax.rsqrt(ms + RMS_EPS) * ng_ref[:, cols]
        o_ref[:, cols] = y.astype(o_ref.dtype)


def _ssd_call(xbc_act, dt_raw, dt_bias, a_log, e_map, *, second, y_fwd=None, z_act=None, d_exp=None,
              norm_g=None):
    bt, seq, _ = xbc_act.shape
    t = SSD_CHUNK
    nc = seq // t
    cidx = (lambda c: nc - 1 - c) if second else (lambda c: c)
    gn_blocks = SSM_WIDTH // SSM_GN
    in_specs = [
        pl.BlockSpec((None, t, SSM_WIDTH), lambda b, c: (b, cidx(c), 0)),
        pl.BlockSpec((None, t, SSM_GN), lambda b, c: (b, cidx(c), gn_blocks)),
        pl.BlockSpec((None, t, SSM_GN), lambda b, c: (b, cidx(c), gn_blocks + 1)),
        pl.BlockSpec((None, t, 2 * SSM_HEADS), lambda b, c: (b, cidx(c), 0)),
        pl.BlockSpec((1, 2 * SSM_HEADS), lambda b, c: (0, 0)),
        pl.BlockSpec((1, 2 * SSM_HEADS), lambda b, c: (0, 0)),
        pl.BlockSpec((2 * SSM_HEADS, SSM_WIDTH), lambda b, c: (0, 0)),
    ]
    args = [xbc_act, xbc_act, xbc_act, dt_raw, dt_bias, a_log, e_map]
    if second:
        in_specs += [
            pl.BlockSpec((None, t, SSM_WIDTH), lambda b, c: (b, cidx(c), 0)),
            pl.BlockSpec((None, t, SSM_WIDTH), lambda b, c: (b, cidx(c), 0)),
            pl.BlockSpec((1, SSM_WIDTH), lambda b, c: (0, 0)),
            pl.BlockSpec((1, SSM_WIDTH), lambda b, c: (0, 0)),
        ]
        args += [y_fwd, z_act, d_exp, norm_g]
    return pl.pallas_call(
        functools.partial(_ssd_kernel, second=second),
        grid=(bt, nc),
        in_specs=in_specs,
        out_specs=pl.BlockSpec((None, t, SSM_WIDTH), lambda b, c: (b, cidx(c), 0)),
        out_shape=jax.ShapeDtypeStruct((bt, seq, SSM_WIDTH), BF16 if second else F32),
        scratch_shapes=[pltpu.VMEM((SSM_STATE, SSM_WIDTH), F32)],
        compiler_params=_params(("parallel", "arbitrary")),
        name="ssd_pass2" if second else "ssd_pass1",
    )(*args)


def _hy_filter_kernel(fr_ref, w1t_ref, w1c_ref, w1s_ref, b1_ref, w2_ref, b2_ref, w3_ref, b3_ref, fq_ref,
                      w4_ref, dl_ref, ks_ref, kd_ref, *, n):
    tl = ks_ref.shape[0]
    pos = (pl.program_id(0) * tl + lax.broadcasted_iota(jnp.int32, (tl, 1), 0)).astype(F32)
    tt = pos / float(n - 1)
    arg = fr_ref[...] * ((2.0 * math.pi / n) * pos)
    fq = fq_ref[...]
    dot = functools.partial(jnp.dot, preferred_element_type=F32)
    h = tt * w1t_ref[...] + dot(jnp.cos(arg), w1c_ref[...]) + dot(-jnp.sin(arg), w1s_ref[...]) + b1_ref[...]
    h = jnp.sin(fq * h)
    h = jnp.sin(fq * (dot(h, w2_ref[...]) + b2_ref[...]))
    h = jnp.sin(fq * (dot(h, w3_ref[...]) + b3_ref[...]))
    k = dot(h, w4_ref[...])
    win = jnp.exp(-tt * dl_ref[...])
    kf = k[:, :HY_WIDTH] * win
    kb = jnp.where(pos == 0.0, 0.0, k[:, HY_WIDTH:] * win)
    ks_ref[...] = (kf + kb).astype(ks_ref.dtype)
    kd_ref[...] = (kf - kb).astype(kd_ref.dtype)


def _hy_filters(n, w1, b1, w2, b2, w3, b3, freq, w4):
    tl = _tile(n, 256)
    fr = jnp.linspace(1e-4, HY_BANDS - 1, HY_BANDS, dtype=F32).reshape(1, HY_BANDS)
    min_decay = math.log(HY_TARGET) / HY_SLOW_DECAY
    max_decay = math.log(HY_TARGET) / HY_FAST_DECAY
    deltas = jnp.abs(jnp.linspace(min_decay, max_decay, HY_WIDTH, dtype=F32)).reshape(1, HY_WIDTH)
    w1 = w1.astype(F32)
    small = [fr, w1[0:1], w1[1:1 + HY_BANDS], w1[1 + HY_BANDS:], b1.reshape(1, -1), w2, b2.reshape(1, -1),
             w3, b3.reshape(1, -1), freq.reshape(1, -1), w4, deltas]
    small = [s.astype(F32) for s in small]
    out_spec = pl.BlockSpec((tl, HY_WIDTH), lambda i: (i, 0))
    return pl.pallas_call(
        functools.partial(_hy_filter_kernel, n=n),
        grid=(n // tl,),
        in_specs=[pl.BlockSpec(s.shape, lambda i: (0, 0)) for s in small],
        out_specs=[out_spec, out_spec],
        out_shape=[jax.ShapeDtypeStruct((n, HY_WIDTH), BF16)] * 2,
        compiler_params=_params(("parallel",)),
        name="hy_filter",
    )(*small)


def _dft_angles(m, n):
    m = jnp.bitwise_and(m, 4 * n - 1)
    m = jnp.where(m >= 2 * n, m - 4 * n, m)
    return m.astype(F32) * (math.pi / (2 * n))


def _dft_gen_kernel(c_ref, s_ref, cb_ref, sb_ref, *, n, inverse):
    tm, tn = c_ref.shape
    r0 = pl.program_id(1) * tm
    di = lax.broadcasted_iota(jnp.int32, (tm, tn), 0)
    c = pl.program_id(0) * tn + lax.broadcasted_iota(jnp.int32, (tm, tn), 1)
    c1 = pl.program_id(0) * tn + lax.broadcasted_iota(jnp.int32, (1, tn), 1)

    @pl.when(pl.program_id(1) == 0)
    def _():
        beta = _dft_angles((2 * c + 1) * di if inverse else 2 * di * c, n)
        cb_ref[...] = jnp.cos(beta)
        sb_ref[...] = jnp.sin(beta)

    alpha = _dft_angles((2 * c1 + 1) * r0 if inverse else (2 * r0 + 1) * c1, n)
    ca, sa = jnp.cos(alpha), jnp.sin(alpha)
    cb, sb = cb_ref[...], sb_ref[...]
    scale = (1.0 / n) if inverse else 1.0
    c_ref[...] = ((ca * cb - sa * sb) * scale).astype(c_ref.dtype)
    s_ref[...] = ((sa * cb + ca * sb) * (-scale)).astype(s_ref.dtype)


def _dft_matrices(n, inverse):
    tm, tn = _tile(n, 256), _tile(n, 512)
    spec = pl.BlockSpec((tm, tn), lambda j, i: (i, j))
    return pl.pallas_call(
        functools.partial(_dft_gen_kernel, n=n, inverse=inverse),
        grid=(n // tn, n // tm),
        out_specs=[spec, spec],
        out_shape=[jax.ShapeDtypeStruct((n, n), BF16)] * 2,
        scratch_shapes=[pltpu.VMEM((tm, tn), F32), pltpu.VMEM((tm, tn), F32)],
        compiler_params=_params(("parallel", "arbitrary")),
        name="dft_gen_inv" if inverse else "dft_gen_fwd",
    )()


def _hy_fwd_kernel(co_ref, sn_ref, w_ref, kre_ref, kim_ref, yre_ref, yim_ref):
    w = w_ref[...]
    wre = jnp.dot(co_ref[...], w, preferred_element_type=F32)
    wim = jnp.dot(sn_ref[...], w, preferred_element_type=F32)
    kre, kim = kre_ref[...], kim_ref[...]
    yre_ref[...] = (wre * kre - wim * kim).astype(yre_ref.dtype)
    yim_ref[...] = (wre * kim + wim * kre).astype(yim_ref.dtype)


def _hy_fwd_call(co, sn, w, kre, kim):
    bt, n, width = w.shape
    tm, tn = _tile(n, 512), _tile(width, 512)
    fspec = pl.BlockSpec((tm, n), lambda i, b, j: (i, 0))
    kspec = pl.BlockSpec((tm, tn), lambda i, b, j: (i, j))
    ospec = pl.BlockSpec((None, tm, tn), lambda i, b, j: (b, i, j))
    return pl.pallas_call(
        _hy_fwd_kernel,
        grid=(n // tm, bt, width // tn),
        in_specs=[fspec, fspec, pl.BlockSpec((None, n, tn), lambda i, b, j: (b, 0, j)), kspec, kspec],
        out_specs=[ospec, ospec],
        out_shape=[jax.ShapeDtypeStruct((bt, n, width), BF16)] * 2,
        compiler_params=_params(("parallel", "parallel", "parallel")),
        name="hy_dft_fwd",
    )(co, sn, w, kre, kim)


def _hy_inv_kernel(ct_ref, st_ref, yre_ref, yim_ref, x0_ref, w_ref, bias_ref, o_ref):
    yc = (jnp.dot(ct_ref[...], yre_ref[...], preferred_element_type=F32)
          + jnp.dot(st_ref[...], yim_ref[...], preferred_element_type=F32))
    o_ref[...] = (x0_ref[...].astype(F32) * (yc + w_ref[...].astype(F32) * bias_ref[...])).astype(o_ref.dtype)


def _hy_inv_call(ct, st, yre, yim, x0, w, bias):
    bt, n, width = w.shape
    tm, tn = _tile(n, 512), _tile(width, 512)
    fspec = pl.BlockSpec((tm, n), lambda i, b, j: (i, 0))
    yspec = pl.BlockSpec((None, n, tn), lambda i, b, j: (b, 0, j))
    tspec = pl.BlockSpec((None, tm, tn), lambda i, b, j: (b, i, j))
    return pl.pallas_call(
        _hy_inv_kernel,
        grid=(n // tm, bt, width // tn),
        in_specs=[fspec, fspec, yspec, yspec, tspec, tspec, pl.BlockSpec((1, tn), lambda i, b, j: (0, j))],
        out_specs=tspec,
        out_shape=jax.ShapeDtypeStruct((bt, n, width), BF16),
        compiler_params=_params(("parallel", "parallel", "parallel")),
        name="hy_dft_inv",
    )(ct, st, yre, yim, x0, w, bias.reshape(1, width).astype(F32))


def _sgu_kernel(x_ref, lng_ref, lnb_ref, ws_ref, bse_ref, o_ref):
    tl = x_ref.shape[0]
    v = x_ref[:, SG_WIDTH:].astype(F32)
    mu = jnp.mean(v, axis=-1, keepdims=True)
    vc = v - mu
    var = jnp.mean(vc * vc, axis=-1, keepdims=True)
    vn = (vc * lax.rsqrt(var + LN_EPS) * lng_ref[...] + lnb_ref[...]).astype(BF16)
    for g in range(SG_GROUPS):
        cols = slice(g * SG_GROUP_DIM, (g + 1) * SG_GROUP_DIM)
        wsg = ws_ref[g]
        for k in range(tl // SG_CHUNK):
            rows = slice(k * SG_CHUNK, (k + 1) * SG_CHUNK)
            mixed = jnp.dot(wsg, vn[rows, cols], preferred_element_type=F32) + bse_ref[:, cols]
            o_ref[rows, cols] = (x_ref[rows, cols].astype(F32) * mixed).astype(o_ref.dtype)


def _sgu_call(sg_act, ln_g, ln_b, ws, bs):
    bt, seq, _ = sg_act.shape
    tl = _tile(seq, 2 * SG_CHUNK)
    bs_exp = jnp.repeat(bs.astype(F32).T, SG_GROUP_DIM, axis=1)
    return pl.pallas_call(
        _sgu_kernel,
        grid=(bt, seq // tl),
        in_specs=[pl.BlockSpec((None, tl, SG_IN), lambda b, i: (b, i, 0)),
                  pl.BlockSpec((1, SG_WIDTH), lambda b, i: (0, 0)),
                  pl.BlockSpec((1, SG_WIDTH), lambda b, i: (0, 0)),
                  pl.BlockSpec((SG_GROUPS, SG_CHUNK, SG_CHUNK), lambda b, i: (0, 0, 0)),
                  pl.BlockSpec((SG_CHUNK, SG_WIDTH), lambda b, i: (0, 0))],
        out_specs=pl.BlockSpec((None, tl, SG_WIDTH), lambda b, i: (b, i, 0)),
        out_shape=jax.ShapeDtypeStruct((bt, seq, SG_WIDTH), BF16),
        compiler_params=_params(("parallel", "parallel")),
        name="sgu",
    )(sg_act, ln_g.reshape(1, -1).astype(F32), ln_b.reshape(1, -1).astype(F32), ws.astype(BF16), bs_exp)


def _merge_kernel(ym_ref, yh_ref, yg_ref, w0_ref, w1_ref, w2_ref, g0_ref, g1_ref, g2_ref, o_ref):
    dot = functools.partial(jnp.dot, preferred_element_type=F32)
    tm = o_ref.shape[0]
    pm = min(tm, MM_PIECE_ROWS)
    for p in range(tm // pm):
        rows = slice(p * pm, (p + 1) * pm)
        acc = g0_ref[rows, :].astype(F32) * dot(ym_ref[rows, :], w0_ref[0])
        acc = acc + g1_ref[rows, :].astype(F32) * dot(yh_ref[rows, :], w1_ref[0])
        acc = acc + g2_ref[rows, :].astype(F32) * dot(yg_ref[rows, :], w2_ref[0])
        o_ref[rows, :] = acc.astype(o_ref.dtype)


def _merge_call(y_m, y_h, y_g, w_br, layer, gates):
    m = y_m.shape[0]
    tm, tn = _tile(m, 512), 512
    nb = D_MODEL // tn
    aspec = lambda width: pl.BlockSpec((tm, width), lambda i, j: (i, 0))
    wspec = lambda row0, rows: _wspec(WSlice(w_br, layer, row0, rows, 0), tn, lambda i, j: j)
    gspec = lambda br: pl.BlockSpec((tm, tn), lambda i, j: (i, br * nb + j))
    return pl.pallas_call(
        _merge_kernel,
        grid=(m // tm, nb),
        in_specs=[aspec(SSM_WIDTH), aspec(HY_WIDTH), aspec(SG_WIDTH), wspec(0, SSM_WIDTH),
                  wspec(SSM_WIDTH, HY_WIDTH), wspec(SSM_WIDTH + HY_WIDTH, SG_WIDTH), gspec(0), gspec(1), gspec(2)],
        out_specs=pl.BlockSpec((tm, tn), lambda i, j: (i, j)),
        out_shape=jax.ShapeDtypeStruct((m, D_MODEL), BF16),
        compiler_params=_params(("parallel", "parallel")),
        name="merge",
    )(y_m, y_h, y_g, w_br, w_br, w_br, gates, gates, gates)


def _prep_layer(l, p):
    heads = jnp.arange(2 * SSM_HEADS)[:, None]
    chan_head = (jnp.arange(SSM_WIDTH) // SSM_HEAD_DIM)[None, :]
    return dict(
        layer=l,
        w_in=p["w_in"], w_br=p["w_br"], w_out=p["w_out"], w_up=p["w_up"], w_down=p["w_down"],
        norm1_g=p["norm1_g"][l], norm2_g=p["norm2_g"][l],
        b_gate=p["b_gate"][l].reshape(1, -1).astype(F32),
        ssm_conv_w=p["ssm_conv_w"][l], ssm_conv_b=p["ssm_conv_b"][l],
        dt_bias=p["ssm_dt_bias"][l].reshape(1, -1).astype(F32),
        a_log=p["ssm_a_log"][l].reshape(1, -1).astype(F32),
        e_fwd=(heads == chan_head).astype(BF16), e_bwd=(heads == chan_head + SSM_HEADS).astype(BF16),
        d_exp=jnp.repeat(p["ssm_d"][l].astype(F32), SSM_HEAD_DIM).reshape(1, -1),
        ssm_norm_g=p["ssm_norm_g"][l].reshape(1, -1).astype(F32),
        hy_conv_w=p["hy_conv_w"][l], hy_conv_b=p["hy_conv_b"][l],
        hy_mlp=tuple(p[k][l] for k in ("hy_w1", "hy_b1", "hy_w2", "hy_b2", "hy_w3", "hy_b3", "hy_freq", "hy_w4")),
        hy_bias=p["hy_bias"][l],
        sg_ln_g=p["sg_ln_g"][l], sg_ln_b=p["sg_ln_b"][l], sg_ws=p["sg_ws"][l], sg_bs=p["sg_bs"][l],
        ffn_conv_w=p["ffn_conv_w"][l], ffn_conv_b=p["ffn_conv_b"][l],
    )


W_IN_COLS = dict(z=0, xbc=SSM_WIDTH, dt=SSM_WIDTH + SSM_XBC, hy=SSM_IN, sg=SSM_IN + HY_IN,
                 gate=SSM_IN + HY_IN + SG_IN)


def _w_in(lw, group, extra=0):
    return WSlice(lw["w_in"], lw["layer"], 0, D_MODEL, W_IN_COLS[group] + extra)


def _ssd_branch(lw, h, dt_raw, z_act, bt, seq, tm=1024):
    xbc_act = _mm_conv(h, [(_w_in(lw, "xbc"), lw["ssm_conv_w"], lw["ssm_conv_b"], 0)], seq=seq,
                       width=SSM_XBC, combine=lambda c: (_silu(c),), n_out=1, tm=tm, tn=512,
                       piece_rows=2 * MM_CONV_PIECE_ROWS, name="in_xbc_conv")[0].reshape(bt, seq, SSM_XBC)
    dt3 = dt_raw.reshape(bt, seq, 2 * SSM_HEADS)
    y_fwd = _ssd_call(xbc_act, dt3, lw["dt_bias"], lw["a_log"], lw["e_fwd"], second=False)
    y_m = _ssd_call(xbc_act, dt3, lw["dt_bias"], lw["a_log"], lw["e_bwd"], second=True, y_fwd=y_fwd,
                    z_act=z_act.reshape(bt, seq, SSM_WIDTH), d_exp=lw["d_exp"], norm_g=lw["ssm_norm_g"])
    return y_m.reshape(bt * seq, SSM_WIDTH)


def _hyena_branch(lw, h, dft, bt, seq, tm=1024):
    co, sn, ct, st = dft
    groups = [(_w_in(lw, "hy", off), lw["hy_conv_w"], lw["hy_conv_b"], off) for off in (0, HY_WIDTH, 2 * HY_WIDTH)]
    x0, w = _mm_conv(h, groups, seq=seq, width=HY_WIDTH, combine=lambda c0, c1, c2: (c0, c2 * c1),
                     n_out=2, tm=tm, tn=256, name="in_hy_conv")
    x0 = x0.reshape(bt, seq, HY_WIDTH)
    w = w.reshape(bt, seq, HY_WIDTH)
    ks, kd = _hy_filters(seq, *lw["hy_mlp"])
    kre = _matmul(co, WSlice(ks[None], 0, 0, seq, 0), HY_WIDTH, out_dtype=F32, name="hy_kre", tm=512, tn=512)
    kim = _matmul(sn, WSlice(kd[None], 0, 0, seq, 0), HY_WIDTH, out_dtype=F32, name="hy_kim", tm=512, tn=512)
    yre, yim = _hy_fwd_call(co, sn, w, kre, kim)
    y_h = _hy_inv_call(ct, st, yre, yim, x0, w, lw["hy_bias"])
    return y_h.reshape(bt * seq, HY_WIDTH)


def _layer(x, lw, dft, bt, seq, tm=1024):
    layer = lw["layer"]
    h = _rmsnorm(x, lw["norm1_g"], BF16)
    z_act = _matmul(h, _w_in(lw, "z"), SSM_WIDTH, out_dtype=BF16, name="in_z", epilogue=_ep_silu)
    dt_raw = _matmul(h, _w_in(lw, "dt"), 2 * SSM_HEADS, out_dtype=F32, name="in_dt")
    sg_act = _matmul(h, _w_in(lw, "sg"), SG_IN, out_dtype=BF16, name="in_sg", epilogue=_ep_gelu)
    gates = _matmul(h, _w_in(lw, "gate"), GATE_IN, out_dtype=BF16, name="in_gate", epilogue=_ep_gate,
                    rows=(lw["b_gate"],))
    y_m = _ssd_branch(lw, h, dt_raw, z_act, bt, seq, tm)
    y_h = _hyena_branch(lw, h, dft, bt, seq, tm)
    y_g = _sgu_call(sg_act.reshape(bt, seq, SG_IN), lw["sg_ln_g"], lw["sg_ln_b"], lw["sg_ws"],
                    lw["sg_bs"]).reshape(bt * seq, SG_WIDTH)
    merged = _merge_call(y_m, y_h, y_g, lw["w_br"], layer, gates)
    x = _matmul(merged, WSlice(lw["w_out"], layer, 0, D_MODEL, 0), D_MODEL, out_dtype=F32, name="out_proj",
                epilogue=_ep_residual, tiles=(x,))
    h2 = _rmsnorm(x, lw["norm2_g"], BF16)
    groups = [(WSlice(lw["w_up"], layer, 0, D_MODEL, off), lw["ffn_conv_w"], lw["ffn_conv_b"], off)
              for off in (0, D_FF)]
    act = _mm_conv(h2, groups, seq=seq, width=D_FF, combine=lambda g, v: (_silu(g) * v,), n_out=1,
                   tm=tm, tn=256, piece_rows=2 * MM_CONV_PIECE_ROWS, name="ffn_up_conv")[0]
    return _matmul(act, WSlice(lw["w_down"], layer, 0, D_FF, 0), D_MODEL, out_dtype=F32, name="ffn_down",
                   epilogue=_ep_residual, tiles=(x,), tm=512, tn=512)


def _trunk(x, layers, normf_g):
    bt, seq, d = x.shape
    dft = (*_dft_matrices(seq, inverse=False), *_dft_matrices(seq, inverse=True))
    xf = x.reshape(bt * seq, d)
    for lw in layers:
        xf = _layer(xf, lw, dft, bt, seq)
    return _rmsnorm(xf, normf_g, F32).reshape(bt, seq, d)


def kernel(x_prompt, x_sample, norm1_g, w_in, b_gate, ssm_conv_w, ssm_conv_b, ssm_dt_bias, ssm_a_log, ssm_d,
           ssm_norm_g, hy_conv_w, hy_conv_b, hy_w1, hy_b1, hy_w2, hy_b2, hy_w3, hy_b3, hy_freq, hy_w4, hy_bias,
           sg_ln_g, sg_ln_b, sg_ws, sg_bs, w_br, w_out, norm2_g, w_up, ffn_conv_w, ffn_conv_b, w_down, normf_g):
    bf = lambda w: w.astype(BF16)
    p = dict(norm1_g=norm1_g, w_in=bf(w_in), b_gate=b_gate, ssm_conv_w=ssm_conv_w, ssm_conv_b=ssm_conv_b,
             ssm_dt_bias=ssm_dt_bias, ssm_a_log=ssm_a_log, ssm_d=ssm_d, ssm_norm_g=ssm_norm_g,
             hy_conv_w=hy_conv_w, hy_conv_b=hy_conv_b, hy_w1=hy_w1, hy_b1=hy_b1, hy_w2=hy_w2, hy_b2=hy_b2,
             hy_w3=hy_w3, hy_b3=hy_b3, hy_freq=hy_freq, hy_w4=hy_w4, hy_bias=hy_bias, sg_ln_g=sg_ln_g,
             sg_ln_b=sg_ln_b, sg_ws=sg_ws, sg_bs=sg_bs, w_br=bf(w_br), w_out=bf(w_out), norm2_g=norm2_g,
             w_up=bf(w_up), ffn_conv_w=ffn_conv_w, ffn_conv_b=ffn_conv_b, w_down=bf(w_down))
    layers = [_prep_layer(l, p) for l in range(w_in.shape[0])]
    return (_trunk(x_prompt, layers, normf_g), _trunk(x_sample, layers, normf_g))
```

```python
import functools
import math
from typing import NamedTuple

import jax
import jax.numpy as jnp
from jax import lax
from jax.experimental import pallas as pl
from jax.experimental.pallas import tpu as pltpu

F32 = jnp.float32
BF16 = jnp.bfloat16

D_MODEL = 4096
SSM_WIDTH = D_MODEL
SSM_HEAD_DIM = 64
SSM_HEADS = SSM_WIDTH // SSM_HEAD_DIM
SSM_GROUPS = 8
SSM_STATE = 128
SSM_CONV = 5
SSD_CHUNK = 128
SSM_GN = SSM_GROUPS * SSM_STATE
SSM_XBC = SSM_WIDTH + 2 * SSM_GN
SSM_IN = SSM_WIDTH + SSM_XBC + 2 * SSM_HEADS
SSM_GROUP_WIDTH = SSM_WIDTH // SSM_GROUPS
HY_WIDTH = D_MODEL // 2
HY_SHORT = 3
HY_EMB = 33
HY_BANDS = (HY_EMB - 1) // 2
HY_ORDER = 64
HY_FAST_DECAY = 0.3
HY_SLOW_DECAY = 1.5
HY_TARGET = 1e-2
HY_IN = 3 * HY_WIDTH
SG_WIDTH = D_MODEL // 2
SG_CHUNK = 128
SG_GROUPS = 16
SG_GROUP_DIM = SG_WIDTH // SG_GROUPS
SG_IN = 2 * SG_WIDTH
N_BRANCH = 3
GATE_IN = N_BRANCH * D_MODEL
D_FF = 256 * math.ceil(8 * D_MODEL / 3 / 256)
FFN_CONV = 3
RMS_EPS = 1e-6
LOG2_E = 1.0 / math.log(2.0)
LN_EPS = 1e-5

V7X_SCOPED_VMEM_BYTES = 56 * 1024 * 1024
LANES = 128
SUBLANES_BF16 = 16
HALO_KEEP = 8
MM_CONV_PIECE_ROWS = 128
MM_PIECE_ROWS = 256
HY_DFT_TILE_ELEMS = 2 ** 21


def _tile(dim, pref):
    t = pref
    while dim % t:
        t //= 2
    return t


def _params(semantics):
    return pltpu.CompilerParams(dimension_semantics=semantics, vmem_limit_bytes=V7X_SCOPED_VMEM_BYTES)


def _silu(x):
    return x * jax.nn.sigmoid(x)


class WSlice(NamedTuple):
    arr: jax.Array
    layer: int
    row0: int
    rows: int
    col0: int


def _wspec(w, tn, col_index):
    return pl.BlockSpec((pl.Element(1), pl.Element(w.rows), pl.Element(tn)),
                        lambda *ids: (w.layer, w.row0, pl.multiple_of(w.col0 + tn * col_index(*ids), LANES)))


def _rmsnorm_kernel(x_ref, g_ref, o_ref):
    x = x_ref[...]
    ms = jnp.mean(x * x, axis=-1, keepdims=True)
    o_ref[...] = (x * lax.rsqrt(ms + RMS_EPS) * g_ref[...]).astype(o_ref.dtype)


def _rmsnorm(x, g, out_dtype):
    m, d = x.shape
    tr = _tile(m, 256)
    return pl.pallas_call(
        _rmsnorm_kernel,
        grid=(m // tr,),
        in_specs=[pl.BlockSpec((tr, d), lambda i: (i, 0)), pl.BlockSpec((1, d), lambda i: (0, 0))],
        out_specs=pl.BlockSpec((tr, d), lambda i: (i, 0)),
        out_shape=jax.ShapeDtypeStruct((m, d), out_dtype),
        compiler_params=_params(("parallel",)),
        name="rmsnorm",
    )(x, g.reshape(1, d).astype(F32))


def _mm_kernel(a_ref, b_ref, *rest, n_rows, n_tiles, epilogue):
    row_refs = rest[:n_rows]
    tile_refs = rest[n_rows:n_rows + n_tiles]
    o_ref = rest[n_rows + n_tiles]
    tm = a_ref.shape[0]
    pm = min(tm, MM_PIECE_ROWS)
    for p in range(tm // pm):
        rows = slice(p * pm, (p + 1) * pm)
        acc = jnp.dot(a_ref[rows, :], b_ref[0], preferred_element_type=F32)
        if epilogue is not None:
            acc = epilogue(acc, *[r[...] for r in row_refs], *[t[rows, :] for t in tile_refs])
        o_ref[rows, :] = acc.astype(o_ref.dtype)


def _matmul(a, w, n, *, out_dtype, name, epilogue=None, rows=(), tiles=(), tm=1024, tn=1024):
    m, k = a.shape
    tm, tn = _tile(m, tm), _tile(n, tn)
    in_specs = [pl.BlockSpec((tm, k), lambda i, j: (i, 0)), _wspec(w, tn, lambda i, j: j)]
    in_specs += [pl.BlockSpec((1, tn), lambda i, j: (0, j)) for _ in rows]
    in_specs += [pl.BlockSpec((tm, tn), lambda i, j: (i, j)) for _ in tiles]
    return pl.pallas_call(
        functools.partial(_mm_kernel, n_rows=len(rows), n_tiles=len(tiles), epilogue=epilogue),
        grid=(m // tm, n // tn),
        in_specs=in_specs,
        out_specs=pl.BlockSpec((tm, tn), lambda i, j: (i, j)),
        out_shape=jax.ShapeDtypeStruct((m, n), out_dtype),
        compiler_params=_params(("parallel", "parallel")),
        name=name,
    )(a, w.arr, *rows, *tiles)


def _ep_silu(acc):
    return _silu(acc)


def _ep_gelu(acc):
    return 0.5 * acc * (1.0 + lax.erf(acc * (1.0 / math.sqrt(2.0))))


def _ep_gate(acc, bias):
    return jax.nn.sigmoid(acc + bias)


def _ep_residual(acc, res):
    return acc + res


def _mm_conv_kernel(a_ref, *refs, n_groups, ksize, n_row_tiles, tiles_per_seq, piece_rows, combine):
    w_refs = refs[:n_groups]
    cw_refs = refs[n_groups:2 * n_groups]
    cb_refs = refs[2 * n_groups:3 * n_groups]
    out_refs = refs[3 * n_groups:-2]
    p_ref, carry_ref = refs[-2:]
    s = pl.program_id(0)
    tm = a_ref.shape[0]
    pad = ksize // 2

    @pl.when(s == 0)
    def _():
        p_ref[...] = jnp.zeros_like(p_ref)
        carry_ref[...] = jnp.zeros_like(carry_ref)

    prev_tile = jnp.maximum(s - 1, 0) % n_row_tiles
    has_prev = jnp.where(prev_tile % tiles_per_seq != 0, 1.0, 0.0)
    has_next = jnp.where(prev_tile % tiles_per_seq != tiles_per_seq - 1, 1.0, 0.0)
    pm = min(tm, piece_rows)
    n_pieces = tm // pm
    cur = [[None] * n_pieces for _ in range(n_groups)]
    for k in range(n_pieces):
        rows = slice(k * pm, (k + 1) * pm)
        a = a_ref[rows, :]
        for g in range(n_groups):
            cur[g][k] = jnp.dot(a, w_refs[g][0], preferred_element_type=F32)
        conv = []
        for g in range(n_groups):
            head = carry_ref[g] * has_prev if k == 0 else p_ref[g, k * pm - HALO_KEEP:k * pm, :]
            tail = (cur[g][0][:HALO_KEEP] * has_next if k == n_pieces - 1
                    else p_ref[g, (k + 1) * pm:(k + 1) * pm + HALO_KEEP, :])
            ext = jnp.concatenate([head, p_ref[g, rows, :], tail], axis=0)
            w = cw_refs[g][...]
            acc = cb_refs[g][...]
            for j in range(ksize):
                off = HALO_KEEP - pad + j
                acc = acc + w[j:j + 1, :] * ext[off:off + pm]
            conv.append(acc)
        for o_ref, val in zip(out_refs, combine(*conv)):
            o_ref[rows, :] = val.astype(o_ref.dtype)
    for g in range(n_groups):
        carry_ref[g] = p_ref[g, tm - HALO_KEEP:, :]
        for k in range(n_pieces):
            p_ref[g, k * pm:(k + 1) * pm, :] = cur[g][k]


def _mm_conv(a, groups, *, seq, width, combine, n_out, tm, tn, name, piece_rows=MM_CONV_PIECE_ROWS):
    t, k = a.shape
    ksize = groups[0][1].shape[0]
    tm, tn = _tile(seq, tm), _tile(width, tn)
    n_row_tiles, n_col_tiles = t // tm, width // tn
    steps = n_row_tiles * n_col_tiles
    n_groups = len(groups)
    cur_tile = lambda s: jnp.minimum(s, steps - 1)
    fin_tile = lambda s: jnp.maximum(s - 1, 0)
    c_offs = [g[3] // tn for g in groups]
    in_specs = [pl.BlockSpec((tm, k), lambda s: (cur_tile(s) % n_row_tiles, 0))]
    in_specs += [_wspec(g[0], tn, lambda s: cur_tile(s) // n_row_tiles) for g in groups]
    in_specs += [pl.BlockSpec((ksize, tn), lambda s, o=o: (0, o + fin_tile(s) // n_row_tiles)) for o in c_offs]
    in_specs += [pl.BlockSpec((1, tn), lambda s, o=o: (0, o + fin_tile(s) // n_row_tiles)) for o in c_offs]
    out_spec = pl.BlockSpec((tm, tn), lambda s: (fin_tile(s) % n_row_tiles, fin_tile(s) // n_row_tiles))
    return pl.pallas_call(
        functools.partial(_mm_conv_kernel, n_groups=n_groups, ksize=ksize, n_row_tiles=n_row_tiles,
                          tiles_per_seq=seq // tm, piece_rows=piece_rows, combine=combine),
        grid=(steps + 1,),
        in_specs=in_specs,
        out_specs=[out_spec] * n_out,
        out_shape=[jax.ShapeDtypeStruct((t, width), BF16)] * n_out,
        scratch_shapes=[pltpu.VMEM((n_groups, tm, tn), F32), pltpu.VMEM((n_groups, HALO_KEEP, tn), F32)],
        compiler_params=_params(("arbitrary",)),
        name=name,
    )(a, *[g[0].arr for g in groups], *[g[1].astype(F32) for g in groups],
      *[g[2].reshape(1, -1).astype(F32) for g in groups])


def _split3(x):
    hi = x.astype(BF16)
    r1 = x - hi.astype(F32)
    mid = r1.astype(BF16)
    lo = (r1 - mid.astype(F32)).astype(BF16)
    return hi, mid, lo


def _ssd_kernel(*refs, second):
    if second:
        (xs_ref, b_ref, c_ref, dtr_ref, dtb_ref, alog_ref, e_ref,
         yf_ref, z_ref, dexp_ref, ng_ref, o_ref, h_ref) = refs
    else:
        xs_ref, b_ref, c_ref, dtr_ref, dtb_ref, alog_ref, e_ref, o_ref, h_ref = refs
    t = SSD_CHUNK

    @pl.when(pl.program_id(1) == 0)
    def _():
        h_ref[...] = jnp.zeros_like(h_ref)

    dt = jax.nn.softplus(dtr_ref[...] + dtb_ref[...])
    da = dt * (-jnp.exp(alog_ref[...]))
    row = lax.broadcasted_iota(jnp.int32, (t, t), 0)
    col = lax.broadcasted_iota(jnp.int32, (t, t), 1)
    tri = (row >= col).astype(BF16)
    pieces = _split3(da)
    p_inc = sum(jnp.dot(tri, x, preferred_element_type=F32) for x in pieces)
    total = jnp.sum(da, axis=0, keepdims=True)
    if second:
        p = p_inc - da
        dd = dt * jnp.exp(p)
        ea = jnp.exp(total - p)
    else:
        dd = dt * jnp.exp(total - p_inc)
        ea = jnp.exp(p_inc)
    q = jnp.concatenate([dd, ea], axis=0).astype(BF16)
    etot = _split3(jnp.broadcast_to(jnp.exp(total), (SUBLANES_BF16, 2 * SSM_HEADS)))
    if second:
        tri_t = (col >= row).astype(BF16)
        p_inc_t = sum(lax.dot_general(x, tri_t, (((0,), (0,)), ((), ())), preferred_element_type=F32)
                      for x in pieces)
        dt_t = dt.T
        log2_dt_t = jnp.log2(dt_t)
        col_f = p_inc * LOG2_E
        row_f = p_inc_t * LOG2_E - log2_dt_t
        col_b = p * LOG2_E
        row_b = (p_inc_t - da.T) * LOG2_E + log2_dt_t
        lower = row >= col
        diag = row == col
        lane = lax.broadcasted_iota(jnp.int32, (t, 4 * SSM_HEAD_DIM), 1)

    for g in range(SSM_GROUPS):
        cols = slice(g * SSM_GROUP_WIDTH, (g + 1) * SSM_GROUP_WIDTH)
        scols = slice(g * SSM_STATE, (g + 1) * SSM_STATE)
        eg = e_ref[:, cols]
        ex = jnp.dot(q, eg, preferred_element_type=F32)
        dd_e, ea_e = ex[0:t], ex[t:2 * t]
        etot_e = sum(jnp.dot(x, eg, preferred_element_type=F32) for x in etot)[0:1]
        xs_b = xs_ref[:, cols]
        xs = xs_b.astype(F32)
        xc2 = (xs * dd_e).astype(BF16)
        bg = b_ref[:, scols]
        cg = c_ref[:, scols]
        h_in = h_ref[:, cols]
        y = jnp.dot(cg, h_in.astype(BF16), preferred_element_type=F32) * ea_e
        s_new = lax.dot_general(bg, xc2, (((0,), (0,)), ((), ())), preferred_element_type=F32)
        h_ref[:, cols] = h_in * etot_e + s_new
        if second:
            cb = lax.dot_general(cg, bg, (((1,), (1,)), ((), ())), preferred_element_type=F32)
            quads = []
            for qd in range(2):
                xq = xs_b[:, qd * 4 * SSM_HEAD_DIM:(qd + 1) * 4 * SSM_HEAD_DIM]
                mats, xms = [], []
                for j in range(4):
                    hf = g * 8 + qd * 4 + j
                    hb = SSM_HEADS + hf
                    bc = lambda v: jnp.broadcast_to(v, (t, t))
                    seg = jnp.where(lower, bc(col_f[:, hf:hf + 1]) - bc(row_f[hf:hf + 1, :]),
                                    bc(row_b[hb:hb + 1, :]) - bc(col_b[:, hb:hb + 1]))
                    w = jnp.exp2(seg) + jnp.where(diag, bc(dt_t[hb:hb + 1, :]), 0.0)
                    mats.append((cb * w).astype(BF16))
                    in_head = (lane >= j * SSM_HEAD_DIM) & (lane < (j + 1) * SSM_HEAD_DIM)
                    xms.append(jnp.where(in_head, xq, jnp.zeros_like(xq)))
                quads.append(jnp.dot(jnp.concatenate(mats, axis=1), jnp.concatenate(xms, axis=0),
                                     preferred_element_type=F32))
            y = y + jnp.concatenate(quads, axis=1)
            y = yf_ref[:, cols] + y + xs * dexp_ref[:, cols]
            y = y * z_ref[:, cols].astype(F32)
            ms = jnp.mean(y * y, axis=-1, keepdims=True)
            y = y * lax.rsqrt(ms + RMS_EPS) * ng_ref[:, cols]
        o_ref[:, cols] = y.astype(o_ref.dtype)


def _ssd_call(xbc_act, dt_raw, dt_bias, a_log, e_map, *, second, y_fwd=None, z_act=None, d_exp=None,
              norm_g=None):
    bt, seq, _ = xbc_act.shape
    t = SSD_CHUNK
    nc = seq // t
    cidx = (lambda c: nc - 1 - c) if second else (lambda c: c)
    gn_blocks = SSM_WIDTH // SSM_GN
    in_specs = [
        pl.BlockSpec((None, t, SSM_WIDTH), lambda b, c: (b, cidx(c), 0)),
        pl.BlockSpec((None, t, SSM_GN), lambda b, c: (b, cidx(c), gn_blocks)),
        pl.BlockSpec((None, t, SSM_GN), lambda b, c: (b, cidx(c), gn_blocks + 1)),
        pl.BlockSpec((None, t, 2 * SSM_HEADS), lambda b, c: (b, cidx(c), 0)),
        pl.BlockSpec((1, 2 * SSM_HEADS), lambda b, c: (0, 0)),
        pl.BlockSpec((1, 2 * SSM_HEADS), lambda b, c: (0, 0)),
        pl.BlockSpec((2 * SSM_HEADS, SSM_WIDTH), lambda b, c: (0, 0)),
    ]
    args = [xbc_act, xbc_act, xbc_act, dt_raw, dt_bias, a_log, e_map]
    if second:
        in_specs += [
            pl.BlockSpec((None, t, SSM_WIDTH), lambda b, c: (b, cidx(c), 0)),
            pl.BlockSpec((None, t, SSM_WIDTH), lambda b, c: (b, cidx(c), 0)),
            pl.BlockSpec((1, SSM_WIDTH), lambda b, c: (0, 0)),
            pl.BlockSpec((1, SSM_WIDTH), lambda b, c: (0, 0)),
        ]
        args += [y_fwd, z_act, d_exp, norm_g]
    return pl.pallas_call(
        functools.partial(_ssd_kernel, second=second),
        grid=(bt, nc),
        in_specs=in_specs,
        out_specs=pl.BlockSpec((None, t, SSM_WIDTH), lambda b, c: (b, cidx(c), 0)),
        out_shape=jax.ShapeDtypeStruct((bt, seq, SSM_WIDTH), BF16 if second else F32),
        scratch_shapes=[pltpu.VMEM((SSM_STATE, SSM_WIDTH), F32)],
        compiler_params=_params(("parallel", "arbitrary")),
        name="ssd_pass2" if second else "ssd_pass1",
    )(*args)


def _hy_filter_kernel(fr_ref, w1t_ref, w1c_ref, w1s_ref, b1_ref, w2_ref, b2_ref, w3_ref, b3_ref, fq_ref,
                      w4_ref, dl_ref, ks_ref, kd_ref, *, n):
    tl = ks_ref.shape[0]
    pos = (pl.program_id(0) * tl + lax.broadcasted_iota(jnp.int32, (tl, 1), 0)).astype(F32)
    tt = pos / float(n - 1)
    arg = fr_ref[...] * ((2.0 * math.pi / n) * pos)
    fq = fq_ref[...]
    dot = functools.partial(jnp.dot, preferred_element_type=F32)
    h = tt * w1t_ref[...] + dot(jnp.cos(arg), w1c_ref[...]) + dot(-jnp.sin(arg), w1s_ref[...]) + b1_ref[...]
    h = jnp.sin(fq * h)
    h = jnp.sin(fq * (dot(h, w2_ref[...]) + b2_ref[...]))
    h = jnp.sin(fq * (dot(h, w3_ref[...]) + b3_ref[...]))
    k = dot(h, w4_ref[...])
    win = jnp.exp(-tt * dl_ref[...])
    kf = k[:, :HY_WIDTH] * win
    kb = jnp.where(pos == 0.0, 0.0, k[:, HY_WIDTH:] * win)
    ks_ref[...] = (kf + kb).astype(ks_ref.dtype)
    kd_ref[...] = (kf - kb).astype(kd_ref.dtype)


def _hy_filters(n, w1, b1, w2, b2, w3, b3, freq, w4):
    tl = _tile(n, 256)
    fr = jnp.linspace(1e-4, HY_BANDS - 1, HY_BANDS, dtype=F32).reshape(1, HY_BANDS)
    min_decay = math.log(HY_TARGET) / HY_SLOW_DECAY
    max_decay = math.log(HY_TARGET) / HY_FAST_DECAY
    deltas = jnp.abs(jnp.linspace(min_decay, max_decay, HY_WIDTH, dtype=F32)).reshape(1, HY_WIDTH)
    w1 = w1.astype(F32)
    small = [fr, w1[0:1], w1[1:1 + HY_BANDS], w1[1 + HY_BANDS:], b1.reshape(1, -1), w2, b2.reshape(1, -1),
             w3, b3.reshape(1, -1), freq.reshape(1, -1), w4, deltas]
    small = [s.astype(F32) for s in small]
    out_spec = pl.BlockSpec((tl, HY_WIDTH), lambda i: (i, 0))
    return pl.pallas_call(
        functools.partial(_hy_filter_kernel, n=n),
        grid=(n // tl,),
        in_specs=[pl.BlockSpec(s.shape, lambda i: (0, 0)) for s in small],
        out_specs=[out_spec, out_spec],
        out_shape=[jax.ShapeDtypeStruct((n, HY_WIDTH), BF16)] * 2,
        compiler_params=_params(("parallel",)),
        name="hy_filter",
    )(*small)


def _dft_angles(m, n):
    m = jnp.bitwise_and(m, 4 * n - 1)
    m = jnp.where(m >= 2 * n, m - 4 * n, m)
    return m.astype(F32) * (math.pi / (2 * n))


def _dft_gen_kernel(c_ref, s_ref, cb_ref, sb_ref, *, n, inverse):
    tm, tn = c_ref.shape
    r0 = pl.program_id(1) * tm
    di = lax.broadcasted_iota(jnp.int32, (tm, tn), 0)
    c = pl.program_id(0) * tn + lax.broadcasted_iota(jnp.int32, (tm, tn), 1)
    c1 = pl.program_id(0) * tn + lax.broadcasted_iota(jnp.int32, (1, tn), 1)

    @pl.when(pl.program_id(1) == 0)
    def _():
        beta = _dft_angles((2 * c + 1) * di if inverse else 2 * di * c, n)
        cb_ref[...] = jnp.cos(beta)
        sb_ref[...] = jnp.sin(beta)

    alpha = _dft_angles((2 * c1 + 1) * r0 if inverse else (2 * r0 + 1) * c1, n)
    ca, sa = jnp.cos(alpha), jnp.sin(alpha)
    cb, sb = cb_ref[...], sb_ref[...]
    scale = (1.0 / n) if inverse else 1.0
    c_ref[...] = ((ca * cb - sa * sb) * scale).astype(c_ref.dtype)
    s_ref[...] = ((sa * cb + ca * sb) * (-scale)).astype(s_ref.dtype)


def _dft_matrices(n, inverse):
    tm, tn = _tile(n, 256), _tile(n, 512)
    spec = pl.BlockSpec((tm, tn), lambda j, i: (i, j))
    return pl.pallas_call(
        functools.partial(_dft_gen_kernel, n=n, inverse=inverse),
        grid=(n // tn, n // tm),
        out_specs=[spec, spec],
        out_shape=[jax.ShapeDtypeStruct((n, n), BF16)] * 2,
        scratch_shapes=[pltpu.VMEM((tm, tn), F32), pltpu.VMEM((tm, tn), F32)],
        compiler_params=_params(("parallel", "arbitrary")),
        name="dft_gen_inv" if inverse else "dft_gen_fwd",
    )()


def _hy_fwd_kernel(co_ref, sn_ref, w_ref, kre_ref, kim_ref, yre_ref, yim_ref):
    w = w_ref[...]
    tm = co_ref.shape[0]
    pm = min(tm, MM_PIECE_ROWS)
    for p in range(tm // pm):
        rows = slice(p * pm, (p + 1) * pm)
        wre = jnp.dot(co_ref[rows, :], w, preferred_element_type=F32)
        wim = jnp.dot(sn_ref[rows, :], w, preferred_element_type=F32)
        kre, kim = kre_ref[rows, :], kim_ref[rows, :]
        yre_ref[rows, :] = (wre * kre - wim * kim).astype(yre_ref.dtype)
        yim_ref[rows, :] = (wre * kim + wim * kre).astype(yim_ref.dtype)


def _hy_fwd_call(co, sn, w, kre, kim):
    bt, n, width = w.shape
    tm, tn = _tile(n, max(512, HY_DFT_TILE_ELEMS // n)), _tile(width, 512)
    fspec = pl.BlockSpec((tm, n), lambda i, b, j: (i, 0))
    kspec = pl.BlockSpec((tm, tn), lambda i, b, j: (i, j))
    ospec = pl.BlockSpec((None, tm, tn), lambda i, b, j: (b, i, j))
    return pl.pallas_call(
        _hy_fwd_kernel,
        grid=(n // tm, bt, width // tn),
        in_specs=[fspec, fspec, pl.BlockSpec((None, n, tn), lambda i, b, j: (b, 0, j)), kspec, kspec],
        out_specs=[ospec, ospec],
        out_shape=[jax.ShapeDtypeStruct((bt, n, width), BF16)] * 2,
        compiler_params=_params(("parallel", "parallel", "parallel")),
        name="hy_dft_fwd",
    )(co, sn, w, kre, kim)


def _hy_inv_kernel(ct_ref, st_ref, yre_ref, yim_ref, x0_ref, w_ref, bias_ref, o_ref):
    yre, yim = yre_ref[...], yim_ref[...]
    tm = ct_ref.shape[0]
    pm = min(tm, MM_PIECE_ROWS)
    for p in range(tm // pm):
        rows = slice(p * pm, (p + 1) * pm)
        yc = (jnp.dot(ct_ref[rows, :], yre, preferred_element_type=F32)
              + jnp.dot(st_ref[rows, :], yim, preferred_element_type=F32))
        o_ref[rows, :] = (x0_ref[rows, :].astype(F32)
                          * (yc + w_ref[rows, :].astype(F32) * bias_ref[...])).astype(o_ref.dtype)


def _hy_inv_call(ct, st, yre, yim, x0, w, bias):
    bt, n, width = w.shape
    tm, tn = _tile(n, max(512, HY_DFT_TILE_ELEMS // n)), _tile(width, 512)
    fspec = pl.BlockSpec((tm, n), lambda i, b, j: (i, 0))
    yspec = pl.BlockSpec((None, n, tn), lambda i, b, j: (b, 0, j))
    tspec = pl.BlockSpec((None, tm, tn), lambda i, b, j: (b, i, j))
    return pl.pallas_call(
        _hy_inv_kernel,
        grid=(n // tm, bt, width // tn),
        in_specs=[fspec, fspec, yspec, yspec, tspec, tspec, pl.BlockSpec((1, tn), lambda i, b, j: (0, j))],
        out_specs=tspec,
        out_shape=jax.ShapeDtypeStruct((bt, n, width), BF16),
        compiler_params=_params(("parallel", "parallel", "parallel")),
        name="hy_dft_inv",
    )(ct, st, yre, yim, x0, w, bias.reshape(1, width).astype(F32))


def _sgu_kernel(x_ref, lng_ref, lnb_ref, ws_ref, bse_ref, o_ref):
    tl = x_ref.shape[0]
    v = x_ref[:, SG_WIDTH:].astype(F32)
    mu = jnp.mean(v, axis=-1, keepdims=True)
    vc = v - mu
    var = jnp.mean(vc * vc, axis=-1, keepdims=True)
    vn = (vc * lax.rsqrt(var + LN_EPS) * lng_ref[...] + lnb_ref[...]).astype(BF16)
    for g in range(SG_GROUPS):
        cols = slice(g * SG_GROUP_DIM, (g + 1) * SG_GROUP_DIM)
        wsg = ws_ref[g]
        for k in range(tl // SG_CHUNK):
            rows = slice(k * SG_CHUNK, (k + 1) * SG_CHUNK)
            mixed = jnp.dot(wsg, vn[rows, cols], preferred_element_type=F32) + bse_ref[:, cols]
            o_ref[rows, cols] = (x_ref[rows, cols].astype(F32) * mixed).astype(o_ref.dtype)


def _sgu_call(sg_act, ln_g, ln_b, ws, bs):
    bt, seq, _ = sg_act.shape
    tl = _tile(seq, 2 * SG_CHUNK)
    bs_exp = jnp.repeat(bs.astype(F32).T, SG_GROUP_DIM, axis=1)
    return pl.pallas_call(
        _sgu_kernel,
        grid=(bt, seq // tl),
        in_specs=[pl.BlockSpec((None, tl, SG_IN), lambda b, i: (b, i, 0)),
                  pl.BlockSpec((1, SG_WIDTH), lambda b, i: (0, 0)),
                  pl.BlockSpec((1, SG_WIDTH), lambda b, i: (0, 0)),
                  pl.BlockSpec((SG_GROUPS, SG_CHUNK, SG_CHUNK), lambda b, i: (0, 0, 0)),
                  pl.BlockSpec((SG_CHUNK, SG_WIDTH), lambda b, i: (0, 0))],
        out_specs=pl.BlockSpec((None, tl, SG_WIDTH), lambda b, i: (b, i, 0)),
        out_shape=jax.ShapeDtypeStruct((bt, seq, SG_WIDTH), BF16),
        compiler_params=_params(("parallel", "parallel")),
        name="sgu",
    )(sg_act, ln_g.reshape(1, -1).astype(F32), ln_b.reshape(1, -1).astype(F32), ws.astype(BF16), bs_exp)


def _merge_kernel(ym_ref, yh_ref, yg_ref, w0_ref, w1_ref, w2_ref, g0_ref, g1_ref, g2_ref, o_ref):
    dot = functools.partial(jnp.dot, preferred_element_type=F32)
    tm = o_ref.shape[0]
    pm = min(tm, MM_PIECE_ROWS)
    for p in range(tm // pm):
        rows = slice(p * pm, (p + 1) * pm)
        acc = g0_ref[rows, :].astype(F32) * dot(ym_ref[rows, :], w0_ref[0])
        acc = acc + g1_ref[rows, :].astype(F32) * dot(yh_ref[rows, :], w1_ref[0])
        acc = acc + g2_ref[rows, :].astype(F32) * dot(yg_ref[rows, :], w2_ref[0])
        o_ref[rows, :] = acc.astype(o_ref.dtype)


def _merge_call(y_m, y_h, y_g, w_br, layer, gates):
    m = y_m.shape[0]
    tm, tn = _tile(m, 1024), 256
    nb = D_MODEL // tn
    aspec = lambda width: pl.BlockSpec((tm, width), lambda i, j: (i, 0))
    wspec = lambda row0, rows: _wspec(WSlice(w_br, layer, row0, rows, 0), tn, lambda i, j: j)
    gspec = lambda br: pl.BlockSpec((tm, tn), lambda i, j: (i, br * nb + j))
    return pl.pallas_call(
        _merge_kernel,
        grid=(m // tm, nb),
        in_specs=[aspec(SSM_WIDTH), aspec(HY_WIDTH), aspec(SG_WIDTH), wspec(0, SSM_WIDTH),
                  wspec(SSM_WIDTH, HY_WIDTH), wspec(SSM_WIDTH + HY_WIDTH, SG_WIDTH), gspec(0), gspec(1), gspec(2)],
        out_specs=pl.BlockSpec((tm, tn), lambda i, j: (i, j)),
        out_shape=jax.ShapeDtypeStruct((m, D_MODEL), BF16),
        compiler_params=_params(("parallel", "parallel")),
        name="merge",
    )(y_m, y_h, y_g, w_br, w_br, w_br, gates, gates, gates)


def _prep_layer(l, p):
    heads = jnp.arange(2 * SSM_HEADS)[:, None]
    chan_head = (jnp.arange(SSM_WIDTH) // SSM_HEAD_DIM)[None, :]
    return dict(
        layer=l,
        w_in=p["w_in"], w_br=p["w_br"], w_out=p["w_out"], w_up=p["w_up"], w_down=p["w_down"],
        norm1_g=p["norm1_g"][l], norm2_g=p["norm2_g"][l],
        b_gate=p["b_gate"][l].reshape(1, -1).astype(F32),
        ssm_conv_w=p["ssm_conv_w"][l], ssm_conv_b=p["ssm_conv_b"][l],
        dt_bias=p["ssm_dt_bias"][l].reshape(1, -1).astype(F32),
        a_log=p["ssm_a_log"][l].reshape(1, -1).astype(F32),
        e_fwd=(heads == chan_head).astype(BF16), e_bwd=(heads == chan_head + SSM_HEADS).astype(BF16),
        d_exp=jnp.repeat(p["ssm_d"][l].astype(F32), SSM_HEAD_DIM).reshape(1, -1),
        ssm_norm_g=p["ssm_norm_g"][l].reshape(1, -1).astype(F32),
        hy_conv_w=p["hy_conv_w"][l], hy_conv_b=p["hy_conv_b"][l],
        hy_mlp=tuple(p[k][l] for k in ("hy_w1", "hy_b1", "hy_w2", "hy_b2", "hy_w3", "hy_b3", "hy_freq", "hy_w4")),
        hy_bias=p["hy_bias"][l],
        sg_ln_g=p["sg_ln_g"][l], sg_ln_b=p["sg_ln_b"][l], sg_ws=p["sg_ws"][l], sg_bs=p["sg_bs"][l],
        ffn_conv_w=p["ffn_conv_w"][l], ffn_conv_b=p["ffn_conv_b"][l],
    )


W_IN_COLS = dict(z=0, xbc=SSM_WIDTH, dt=SSM_WIDTH + SSM_XBC, hy=SSM_IN, sg=SSM_IN + HY_IN,
                 gate=SSM_IN + HY_IN + SG_IN)


def _w_in(lw, group, extra=0):
    return WSlice(lw["w_in"], lw["layer"], 0, D_MODEL, W_IN_COLS[group] + extra)


def _ssd_branch(lw, h, dt_raw, z_act, bt, seq, tm=1024):
    xbc_act = _mm_conv(h, [(_w_in(lw, "xbc"), lw["ssm_conv_w"], lw["ssm_conv_b"], 0)], seq=seq,
                       width=SSM_XBC, combine=lambda c: (_silu(c),), n_out=1, tm=tm, tn=512,
                       piece_rows=2 * MM_CONV_PIECE_ROWS, name="in_xbc_conv")[0].reshape(bt, seq, SSM_XBC)
    dt3 = dt_raw.reshape(bt, seq, 2 * SSM_HEADS)
    y_fwd = _ssd_call(xbc_act, dt3, lw["dt_bias"], lw["a_log"], lw["e_fwd"], second=False)
    y_m = _ssd_call(xbc_act, dt3, lw["dt_bias"], lw["a_log"], lw["e_bwd"], second=True, y_fwd=y_fwd,
                    z_act=z_act.reshape(bt, seq, SSM_WIDTH), d_exp=lw["d_exp"], norm_g=lw["ssm_norm_g"])
    return y_m.reshape(bt * seq, SSM_WIDTH)


def _hyena_branch(lw, h, dft, bt, seq, tm=1024):
    co, sn, ct, st = dft
    groups = [(_w_in(lw, "hy", off), lw["hy_conv_w"], lw["hy_conv_b"], off) for off in (0, HY_WIDTH, 2 * HY_WIDTH)]
    x0, w = _mm_conv(h, groups, seq=seq, width=HY_WIDTH, combine=lambda c0, c1, c2: (c0, c2 * c1),
                     n_out=2, tm=tm, tn=256, name="in_hy_conv")
    x0 = x0.reshape(bt, seq, HY_WIDTH)
    w = w.reshape(bt, seq, HY_WIDTH)
    ks, kd = _hy_filters(seq, *lw["hy_mlp"])
    kre = _matmul(co, WSlice(ks[None], 0, 0, seq, 0), HY_WIDTH, out_dtype=F32, name="hy_kre", tm=512, tn=512)
    kim = _matmul(sn, WSlice(kd[None], 0, 0, seq, 0), HY_WIDTH, out_dtype=F32, name="hy_kim", tm=512, tn=512)
    yre, yim = _hy_fwd_call(co, sn, w, kre, kim)
    y_h = _hy_inv_call(ct, st, yre, yim, x0, w, lw["hy_bias"])
    return y_h.reshape(bt * seq, HY_WIDTH)


def _layer(x, lw, dft, bt, seq, tm=1024):
    layer = lw["layer"]
    h = _rmsnorm(x, lw["norm1_g"], BF16)
    z_act = _matmul(h, _w_in(lw, "z"), SSM_WIDTH, out_dtype=BF16, name="in_z", epilogue=_ep_silu)
    dt_raw = _matmul(h, _w_in(lw, "dt"), 2 * SSM_HEADS, out_dtype=F32, name="in_dt")
    sg_act = _matmul(h, _w_in(lw, "sg"), SG_IN, out_dtype=BF16, name="in_sg", epilogue=_ep_gelu)
    gates = _matmul(h, _w_in(lw, "gate"), GATE_IN, out_dtype=BF16, name="in_gate", epilogue=_ep_gate,
                    rows=(lw["b_gate"],))
    y_m = _ssd_branch(lw, h, dt_raw, z_act, bt, seq, tm)
    y_h = _hyena_branch(lw, h, dft, bt, seq, tm)
    y_g = _sgu_call(sg_act.reshape(bt, seq, SG_IN), lw["sg_ln_g"], lw["sg_ln_b"], lw["sg_ws"],
                    lw["sg_bs"]).reshape(bt * seq, SG_WIDTH)
    merged = _merge_call(y_m, y_h, y_g, lw["w_br"], layer, gates)
    x = _matmul(merged, WSlice(lw["w_out"], layer, 0, D_MODEL, 0), D_MODEL, out_dtype=F32, name="out_proj",
                epilogue=_ep_residual, tiles=(x,))
    h2 = _rmsnorm(x, lw["norm2_g"], BF16)
    groups = [(WSlice(lw["w_up"], layer, 0, D_MODEL, off), lw["ffn_conv_w"], lw["ffn_conv_b"], off)
              for off in (0, D_FF)]
    act = _mm_conv(h2, groups, seq=seq, width=D_FF, combine=lambda g, v: (_silu(g) * v,), n_out=1,
                   tm=tm, tn=256, piece_rows=2 * MM_CONV_PIECE_ROWS, name="ffn_up_conv")[0]
    return _matmul(act, WSlice(lw["w_down"], layer, 0, D_FF, 0), D_MODEL, out_dtype=F32, name="ffn_down",
                   epilogue=_ep_residual, tiles=(x,), tm=512, tn=512)


def _trunk(x, layers, normf_g):
    bt, seq, d = x.shape
    dft = (*_dft_matrices(seq, inverse=False), *_dft_matrices(seq, inverse=True))
    xf = x.reshape(bt * seq, d)
    for lw in layers:
        xf = _layer(xf, lw, dft, bt, seq)
    return _rmsnorm(xf, normf_g, F32).reshape(bt, seq, d)


def kernel(x_prompt, x_sample, norm1_g, w_in, b_gate, ssm_conv_w, ssm_conv_b, ssm_dt_bias, ssm_a_log, ssm_d,
           ssm_norm_g, hy_conv_w, hy_conv_b, hy_w1, hy_b1, hy_w2, hy_b2, hy_w3, hy_b3, hy_freq, hy_w4, hy_bias,
           sg_ln_g, sg_ln_b, sg_ws, sg_bs, w_br, w_out, norm2_g, w_up, ffn_conv_w, ffn_conv_b, w_down, normf_g):
    bf = lambda w: w.astype(BF16)
    p = dict(norm1_g=norm1_g, w_in=bf(w_in), b_gate=b_gate, ssm_conv_w=ssm_conv_w, ssm_conv_b=ssm_conv_b,
             ssm_dt_bias=ssm_dt_bias, ssm_a_log=ssm_a_log, ssm_d=ssm_d, ssm_norm_g=ssm_norm_g,
             hy_conv_w=hy_conv_w, hy_conv_b=hy_conv_b, hy_w1=hy_w1, hy_b1=hy_b1, hy_w2=hy_w2, hy_b2=hy_b2,
             hy_w3=hy_w3, hy_b3=hy_b3, hy_freq=hy_freq, hy_w4=hy_w4, hy_bias=hy_bias, sg_ln_g=sg_ln_g,
             sg_ln_b=sg_ln_b, sg_ws=sg_ws, sg_bs=sg_bs, w_br=bf(w_br), w_out=bf(w_out), norm2_g=norm2_g,
             w_up=bf(w_up), ffn_conv_w=ffn_conv_w, ffn_conv_b=ffn_conv_b, w_down=bf(w_down))
    layers = [_prep_layer(l, p) for l in range(w_in.shape[0])]
    return (_trunk(x_prompt, layers, normf_g), _trunk(x_sample, layers, normf_g))
```

```python
import functools
import math
from typing import NamedTuple

import jax
import jax.numpy as jnp
from jax import lax
from jax.experimental import pallas as pl
from jax.experimental.pallas import tpu as pltpu

F32 = jnp.float32
BF16 = jnp.bfloat16

D_MODEL = 4096
SSM_WIDTH = D_MODEL
SSM_HEAD_DIM = 64
SSM_HEADS = SSM_WIDTH // SSM_HEAD_DIM
SSM_GROUPS = 8
SSM_STATE = 128
SSM_CONV = 5
SSD_CHUNK = 128
SSM_GN = SSM_GROUPS * SSM_STATE
SSM_XBC = SSM_WIDTH + 2 * SSM_GN
SSM_IN = SSM_WIDTH + SSM_XBC + 2 * SSM_HEADS
SSM_GROUP_WIDTH = SSM_WIDTH // SSM_GROUPS
HY_WIDTH = D_MODEL // 2
HY_SHORT = 3
HY_EMB = 33
HY_BANDS = (HY_EMB - 1) // 2
HY_ORDER = 64
HY_FAST_DECAY = 0.3
HY_SLOW_DECAY = 1.5
HY_TARGET = 1e-2
HY_IN = 3 * HY_WIDTH
SG_WIDTH = D_MODEL // 2
SG_CHUNK = 128
SG_GROUPS = 16
SG_GROUP_DIM = SG_WIDTH // SG_GROUPS
SG_IN = 2 * SG_WIDTH
N_BRANCH = 3
GATE_IN = N_BRANCH * D_MODEL
D_FF = 256 * math.ceil(8 * D_MODEL / 3 / 256)
FFN_CONV = 3
RMS_EPS = 1e-6
LOG2_E = 1.0 / math.log(2.0)
LN_EPS = 1e-5

V7X_SCOPED_VMEM_BYTES = 56 * 1024 * 1024
LANES = 128
SUBLANES_BF16 = 16
HALO_KEEP = 8
MM_CONV_PIECE_ROWS = 128
MM_PIECE_ROWS = 256
HY_DFT_TILE_ELEMS = 2 ** 21


def _tile(dim, pref):
    t = pref
    while dim % t:
        t //= 2
    return t


def _params(semantics):
    return pltpu.CompilerParams(dimension_semantics=semantics, vmem_limit_bytes=V7X_SCOPED_VMEM_BYTES)


def _silu(x):
    return x * jax.nn.sigmoid(x)


class WSlice(NamedTuple):
    arr: jax.Array
    layer: int
    row0: int
    rows: int
    col0: int


def _wspec(w, tn, col_index):
    return pl.BlockSpec((pl.Element(1), pl.Element(w.rows), pl.Element(tn)),
                        lambda *ids: (w.layer, w.row0, pl.multiple_of(w.col0 + tn * col_index(*ids), LANES)))


def _rmsnorm_kernel(x_ref, g_ref, o_ref):
    x = x_ref[...]
    ms = jnp.mean(x * x, axis=-1, keepdims=True)
    o_ref[...] = (x * lax.rsqrt(ms + RMS_EPS) * g_ref[...]).astype(o_ref.dtype)


def _rmsnorm(x, g, out_dtype):
    m, d = x.shape
    tr = _tile(m, 256)
    return pl.pallas_call(
        _rmsnorm_kernel,
        grid=(m // tr,),
        in_specs=[pl.BlockSpec((tr, d), lambda i: (i, 0)), pl.BlockSpec((1, d), lambda i: (0, 0))],
        out_specs=pl.BlockSpec((tr, d), lambda i: (i, 0)),
        out_shape=jax.ShapeDtypeStruct((m, d), out_dtype),
        compiler_params=_params(("parallel",)),
        name="rmsnorm",
    )(x, g.reshape(1, d).astype(F32))


def _mm_kernel(a_ref, b_ref, *rest, n_rows, n_tiles, epilogue):
    row_refs = rest[:n_rows]
    tile_refs = rest[n_rows:n_rows + n_tiles]
    o_ref = rest[n_rows + n_tiles]
    tm = a_ref.shape[0]
    pm = min(tm, MM_PIECE_ROWS)
    for p in range(tm // pm):
        rows = slice(p * pm, (p + 1) * pm)
        acc = jnp.dot(a_ref[rows, :], b_ref[0], preferred_element_type=F32)
        if epilogue is not None:
            acc = epilogue(acc, *[r[...] for r in row_refs], *[t[rows, :] for t in tile_refs])
        o_ref[rows, :] = acc.astype(o_ref.dtype)


def _matmul(a, w, n, *, out_dtype, name, epilogue=None, rows=(), tiles=(), tm=1024, tn=1024):
    m, k = a.shape
    tm, tn = _tile(m, tm), _tile(n, tn)
    in_specs = [pl.BlockSpec((tm, k), lambda i, j: (i, 0)), _wspec(w, tn, lambda i, j: j)]
    in_specs += [pl.BlockSpec((1, tn), lambda i, j: (0, j)) for _ in rows]
    in_specs += [pl.BlockSpec((tm, tn), lambda i, j: (i, j)) for _ in tiles]
    return pl.pallas_call(
        functools.partial(_mm_kernel, n_rows=len(rows), n_tiles=len(tiles), epilogue=epilogue),
        grid=(m // tm, n // tn),
        in_specs=in_specs,
        out_specs=pl.BlockSpec((tm, tn), lambda i, j: (i, j)),
        out_shape=jax.ShapeDtypeStruct((m, n), out_dtype),
        compiler_params=_params(("parallel", "parallel")),
        name=name,
    )(a, w.arr, *rows, *tiles)


def _ep_silu(acc):
    return _silu(acc)


def _ep_gelu(acc):
    return 0.5 * acc * (1.0 + lax.erf(acc * (1.0 / math.sqrt(2.0))))


def _ep_gate(acc, bias):
    return jax.nn.sigmoid(acc + bias)


def _ep_residual(acc, res):
    return acc + res


def _mm_conv_kernel(a_ref, *refs, n_groups, ksize, n_row_tiles, tiles_per_seq, piece_rows, combine):
    w_refs = refs[:n_groups]
    cw_refs = refs[n_groups:2 * n_groups]
    cb_refs = refs[2 * n_groups:3 * n_groups]
    out_refs = refs[3 * n_groups:-2]
    p_ref, carry_ref = refs[-2:]
    s = pl.program_id(0)
    tm = a_ref.shape[0]
    pad = ksize // 2

    @pl.when(s == 0)
    def _():
        p_ref[...] = jnp.zeros_like(p_ref)
        carry_ref[...] = jnp.zeros_like(carry_ref)

    prev_tile = jnp.maximum(s - 1, 0) % n_row_tiles
    has_prev = prev_tile % tiles_per_seq != 0
    has_next = prev_tile % tiles_per_seq != tiles_per_seq - 1
    pm = min(tm, piece_rows)
    n_pieces = tm // pm
    cur = [[None] * n_pieces for _ in range(n_groups)]
    for k in range(n_pieces):
        rows = slice(k * pm, (k + 1) * pm)
        a = a_ref[rows, :]
        for g in range(n_groups):
            cur[g][k] = jnp.dot(a, w_refs[g][0], preferred_element_type=F32)
        conv = []
        for g in range(n_groups):
            head = (jnp.where(has_prev, carry_ref[g], 0.0) if k == 0
                    else p_ref[g, k * pm - HALO_KEEP:k * pm, :])
            tail = (jnp.where(has_next, cur[g][0][:HALO_KEEP], 0.0) if k == n_pieces - 1
                    else p_ref[g, (k + 1) * pm:(k + 1) * pm + HALO_KEEP, :])
            ext = jnp.concatenate([head, p_ref[g, rows, :], tail], axis=0)
            w = cw_refs[g][...]
            acc = cb_refs[g][...]
            for j in range(ksize):
                off = HALO_KEEP - pad + j
                acc = acc + w[j:j + 1, :] * ext[off:off + pm]
            conv.append(acc)
        for o_ref, val in zip(out_refs, combine(*conv)):
            o_ref[rows, :] = val.astype(o_ref.dtype)
    for g in range(n_groups):
        carry_ref[g] = p_ref[g, tm - HALO_KEEP:, :]
        for k in range(n_pieces):
            p_ref[g, k * pm:(k + 1) * pm, :] = cur[g][k]


def _mm_conv(a, groups, *, seq, width, combine, n_out, tm, tn, name, piece_rows=MM_CONV_PIECE_ROWS):
    t, k = a.shape
    ksize = groups[0][1].shape[0]
    tm, tn = _tile(seq, tm), _tile(width, tn)
    n_row_tiles, n_col_tiles = t // tm, width // tn
    steps = n_row_tiles * n_col_tiles
    n_groups = len(groups)
    cur_tile = lambda s: jnp.minimum(s, steps - 1)
    fin_tile = lambda s: jnp.maximum(s - 1, 0)
    c_offs = [g[3] // tn for g in groups]
    in_specs = [pl.BlockSpec((tm, k), lambda s: (cur_tile(s) % n_row_tiles, 0))]
    in_specs += [_wspec(g[0], tn, lambda s: cur_tile(s) // n_row_tiles) for g in groups]
    in_specs += [pl.BlockSpec((ksize, tn), lambda s, o=o: (0, o + fin_tile(s) // n_row_tiles)) for o in c_offs]
    in_specs += [pl.BlockSpec((1, tn), lambda s, o=o: (0, o + fin_tile(s) // n_row_tiles)) for o in c_offs]
    out_spec = pl.BlockSpec((tm, tn), lambda s: (fin_tile(s) % n_row_tiles, fin_tile(s) // n_row_tiles))
    return pl.pallas_call(
        functools.partial(_mm_conv_kernel, n_groups=n_groups, ksize=ksize, n_row_tiles=n_row_tiles,
                          tiles_per_seq=seq // tm, piece_rows=piece_rows, combine=combine),
        grid=(steps + 1,),
        in_specs=in_specs,
        out_specs=[out_spec] * n_out,
        out_shape=[jax.ShapeDtypeStruct((t, width), BF16)] * n_out,
        scratch_shapes=[pltpu.VMEM((n_groups, tm, tn), F32), pltpu.VMEM((n_groups, HALO_KEEP, tn), F32)],
        compiler_params=_params(("arbitrary",)),
        name=name,
    )(a, *[g[0].arr for g in groups], *[g[1].astype(F32) for g in groups],
      *[g[2].reshape(1, -1).astype(F32) for g in groups])


def _split3(x):
    hi = x.astype(BF16)
    r1 = x - hi.astype(F32)
    mid = r1.astype(BF16)
    lo = (r1 - mid.astype(F32)).astype(BF16)
    return hi, mid, lo


def _ssd_kernel(*refs, second):
    if second:
        (xs_ref, b_ref, c_ref, dtr_ref, dtb_ref, alog_ref, e_ref,
         yf_ref, z_ref, dexp_ref, ng_ref, o_ref, h_ref) = refs
    else:
        xs_ref, b_ref, c_ref, dtr_ref, dtb_ref, alog_ref, e_ref, o_ref, h_ref = refs
    t = SSD_CHUNK

    @pl.when(pl.program_id(1) == 0)
    def _():
        h_ref[...] = jnp.zeros_like(h_ref)

    dt = jax.nn.softplus(dtr_ref[...] + dtb_ref[...])
    da = dt * (-jnp.exp(alog_ref[...]))
    row = lax.broadcasted_iota(jnp.int32, (t, t), 0)
    col = lax.broadcasted_iota(jnp.int32, (t, t), 1)
    tri = (row >= col).astype(BF16)
    pieces = _split3(da)
    p_inc = sum(jnp.dot(tri, x, preferred_element_type=F32) for x in pieces)
    total = jnp.sum(da, axis=0, keepdims=True)
    if second:
        p = p_inc - da
        dd = dt * jnp.exp(p)
        ea = jnp.exp(total - p)
    else:
        dd = dt * jnp.exp(total - p_inc)
        ea = jnp.exp(p_inc)
    q = jnp.concatenate([dd, ea], axis=0).astype(BF16)
    etot = _split3(jnp.broadcast_to(jnp.exp(total), (SUBLANES_BF16, 2 * SSM_HEADS)))
    if second:
        tri_t = (col >= row).astype(BF16)
        p_inc_t = sum(lax.dot_general(x, tri_t, (((0,), (0,)), ((), ())), preferred_element_type=F32)
                      for x in pieces)
        dt_t = dt.T
        log2_dt_t = jnp.log2(dt_t)
        col_f = p_inc * LOG2_E
        row_f = p_inc_t * LOG2_E - log2_dt_t
        col_b = p * LOG2_E
        row_b = (p_inc_t - da.T) * LOG2_E + log2_dt_t
        lower = row >= col
        diag = row == col
        lane = lax.broadcasted_iota(jnp.int32, (t, 4 * SSM_HEAD_DIM), 1)

    for g in range(SSM_GROUPS):
        cols = slice(g * SSM_GROUP_WIDTH, (g + 1) * SSM_GROUP_WIDTH)
        scols = slice(g * SSM_STATE, (g + 1) * SSM_STATE)
        eg = e_ref[:, cols]
        ex = jnp.dot(q, eg, preferred_element_type=F32)
        dd_e, ea_e = ex[0:t], ex[t:2 * t]
        etot_e = sum(jnp.dot(x, eg, preferred_element_type=F32) for x in etot)[0:1]
        xs_b = xs_ref[:, cols]
        xs = xs_b.astype(F32)
        xc2 = (xs * dd_e).astype(BF16)
        bg = b_ref[:, scols]
        cg = c_ref[:, scols]
        h_in = h_ref[:, cols]
        y = jnp.dot(cg, h_in.astype(BF16), preferred_element_type=F32) * ea_e
        s_new = lax.dot_general(bg, xc2, (((0,), (0,)), ((), ())), preferred_element_type=F32)
        h_ref[:, cols] = h_in * etot_e + s_new
        if second:
            cb = lax.dot_general(cg, bg, (((1,), (1,)), ((), ())), preferred_element_type=F32)
            quads = []
            for qd in range(2):
                xq = xs_b[:, qd * 4 * SSM_HEAD_DIM:(qd + 1) * 4 * SSM_HEAD_DIM]
                mats, xms = [], []
                for j in range(4):
                    hf = g * 8 + qd * 4 + j
                    hb = SSM_HEADS + hf
                    bc = lambda v: jnp.broadcast_to(v, (t, t))
                    seg = jnp.where(lower, bc(col_f[:, hf:hf + 1]) - bc(row_f[hf:hf + 1, :]),
                                    bc(row_b[hb:hb + 1, :]) - bc(col_b[:, hb:hb + 1]))
                    w = jnp.exp2(seg) + jnp.where(diag, bc(dt_t[hb:hb + 1, :]), 0.0)
                    mats.append((cb * w).astype(BF16))
                    in_head = (lane >= j * SSM_HEAD_DIM) & (lane < (j + 1) * SSM_HEAD_DIM)
                    xms.append(jnp.where(in_head, xq, jnp.zeros_like(xq)))
                quads.append(jnp.dot(jnp.concatenate(mats, axis=1), jnp.concatenate(xms, axis=0),
                                     preferred_element_type=F32))
            y = y + jnp.concatenate(quads, axis=1)
            y = yf_ref[:, cols] + y + xs * dexp_ref[:, cols]
            y = y * z_ref[:, cols].astype(F32)
            ms = jnp.mean(y * y, axis=-1, keepdims=True)
            y = y * lax.rsqrt(ms + RMS_EPS) * ng_ref[:, cols]
        o_ref[:, cols] = y.astype(o_ref.dtype)


def _ssd_call(xbc_act, dt_raw, dt_bias, a_log, e_map, *, second, y_fwd=None, z_act=None, d_exp=None,
              norm_g=None):
    bt, seq, _ = xbc_act.shape
    t = SSD_CHUNK
    nc = seq // t
    cidx = (lambda c: nc - 1 - c) if second else (lambda c: c)
    gn_blocks = SSM_WIDTH // SSM_GN
    in_specs = [
        pl.BlockSpec((None, t, SSM_WIDTH), lambda b, c: (b, cidx(c), 0)),
        pl.BlockSpec((None, t, SSM_GN), lambda b, c: (b, cidx(c), gn_blocks)),
        pl.BlockSpec((None, t, SSM_GN), lambda b, c: (b, cidx(c), gn_blocks + 1)),
        pl.BlockSpec((None, t, 2 * SSM_HEADS), lambda b, c: (b, cidx(c), 0)),
        pl.BlockSpec((1, 2 * SSM_HEADS), lambda b, c: (0, 0)),
        pl.BlockSpec((1, 2 * SSM_HEADS), lambda b, c: (0, 0)),
        pl.BlockSpec((2 * SSM_HEADS, SSM_WIDTH), lambda b, c: (0, 0)),
    ]
    args = [xbc_act, xbc_act, xbc_act, dt_raw, dt_bias, a_log, e_map]
    if second:
        in_specs += [
            pl.BlockSpec((None, t, SSM_WIDTH), lambda b, c: (b, cidx(c), 0)),
            pl.BlockSpec((None, t, SSM_WIDTH), lambda b, c: (b, cidx(c), 0)),
            pl.BlockSpec((1, SSM_WIDTH), lambda b, c: (0, 0)),
            pl.BlockSpec((1, SSM_WIDTH), lambda b, c: (0, 0)),
        ]
        args += [y_fwd, z_act, d_exp, norm_g]
    return pl.pallas_call(
        functools.partial(_ssd_kernel, second=second),
        grid=(bt, nc),
        in_specs=in_specs,
        out_specs=pl.BlockSpec((None, t, SSM_WIDTH), lambda b, c: (b, cidx(c), 0)),
        out_shape=jax.ShapeDtypeStruct((bt, seq, SSM_WIDTH), BF16 if second else F32),
        scratch_shapes=[pltpu.VMEM((SSM_STATE, SSM_WIDTH), F32)],
        compiler_params=_params(("parallel", "arbitrary")),
        name="ssd_pass2" if second else "ssd_pass1",
    )(*args)


def _hy_filter_kernel(fr_ref, w1t_ref, w1c_ref, w1s_ref, b1_ref, w2_ref, b2_ref, w3_ref, b3_ref, fq_ref,
                      w4_ref, dl_ref, ks_ref, kd_ref, *, n):
    tl = ks_ref.shape[0]
    pos = (pl.program_id(0) * tl + lax.broadcasted_iota(jnp.int32, (tl, 1), 0)).astype(F32)
    tt = pos / float(n - 1)
    arg = fr_ref[...] * ((2.0 * math.pi / n) * pos)
    fq = fq_ref[...]
    dot = functools.partial(jnp.dot, preferred_element_type=F32)
    h = tt * w1t_ref[...] + dot(jnp.cos(arg), w1c_ref[...]) + dot(-jnp.sin(arg), w1s_ref[...]) + b1_ref[...]
    h = jnp.sin(fq * h)
    h = jnp.sin(fq * (dot(h, w2_ref[...]) + b2_ref[...]))
    h = jnp.sin(fq * (dot(h, w3_ref[...]) + b3_ref[...]))
    k = dot(h, w4_ref[...])
    win = jnp.exp(-tt * dl_ref[...])
    kf = k[:, :HY_WIDTH] * win
    kb = jnp.where(pos == 0.0, 0.0, k[:, HY_WIDTH:] * win)
    ks_ref[...] = (kf + kb).astype(ks_ref.dtype)
    kd_ref[...] = (kf - kb).astype(kd_ref.dtype)


def _hy_filters(n, w1, b1, w2, b2, w3, b3, freq, w4):
    tl = _tile(n, 256)
    fr = jnp.linspace(1e-4, HY_BANDS - 1, HY_BANDS, dtype=F32).reshape(1, HY_BANDS)
    min_decay = math.log(HY_TARGET) / HY_SLOW_DECAY
    max_decay = math.log(HY_TARGET) / HY_FAST_DECAY
    deltas = jnp.abs(jnp.linspace(min_decay, max_decay, HY_WIDTH, dtype=F32)).reshape(1, HY_WIDTH)
    w1 = w1.astype(F32)
    small = [fr, w1[0:1], w1[1:1 + HY_BANDS], w1[1 + HY_BANDS:], b1.reshape(1, -1), w2, b2.reshape(1, -1),
             w3, b3.reshape(1, -1), freq.reshape(1, -1), w4, deltas]
    small = [s.astype(F32) for s in small]
    out_spec = pl.BlockSpec((tl, HY_WIDTH), lambda i: (i, 0))
    return pl.pallas_call(
        functools.partial(_hy_filter_kernel, n=n),
        grid=(n // tl,),
        in_specs=[pl.BlockSpec(s.shape, lambda i: (0, 0)) for s in small],
        out_specs=[out_spec, out_spec],
        out_shape=[jax.ShapeDtypeStruct((n, HY_WIDTH), BF16)] * 2,
        compiler_params=_params(("parallel",)),
        name="hy_filter",
    )(*small)


def _dft_angles(m, n):
    m = jnp.bitwise_and(m, 4 * n - 1)
    m = jnp.where(m >= 2 * n, m - 4 * n, m)
    return m.astype(F32) * (math.pi / (2 * n))


def _dft_gen_kernel(c_ref, s_ref, cb_ref, sb_ref, *, n, inverse):
    tm, tn = c_ref.shape
    r0 = pl.program_id(1) * tm
    di = lax.broadcasted_iota(jnp.int32, (tm, tn), 0)
    c = pl.program_id(0) * tn + lax.broadcasted_iota(jnp.int32, (tm, tn), 1)
    c1 = pl.program_id(0) * tn + lax.broadcasted_iota(jnp.int32, (1, tn), 1)

    @pl.when(pl.program_id(1) == 0)
    def _():
        beta = _dft_angles((2 * c + 1) * di if inverse else 2 * di * c, n)
        cb_ref[...] = jnp.cos(beta)
        sb_ref[...] = jnp.sin(beta)

    alpha = _dft_angles((2 * c1 + 1) * r0 if inverse else (2 * r0 + 1) * c1, n)
    ca, sa = jnp.cos(alpha), jnp.sin(alpha)
    cb, sb = cb_ref[...], sb_ref[...]
    scale = (1.0 / n) if inverse else 1.0
    c_ref[...] = ((ca * cb - sa * sb) * scale).astype(c_ref.dtype)
    s_ref[...] = ((sa * cb + ca * sb) * (-scale)).astype(s_ref.dtype)


def _dft_matrices(n, inverse):
    tm, tn = _tile(n, 256), _tile(n, 512)
    spec = pl.BlockSpec((tm, tn), lambda j, i: (i, j))
    return pl.pallas_call(
        functools.partial(_dft_gen_kernel, n=n, inverse=inverse),
        grid=(n // tn, n // tm),
        out_specs=[spec, spec],
        out_shape=[jax.ShapeDtypeStruct((n, n), BF16)] * 2,
        scratch_shapes=[pltpu.VMEM((tm, tn), F32), pltpu.VMEM((tm, tn), F32)],
        compiler_params=_params(("parallel", "arbitrary")),
        name="dft_gen_inv" if inverse else "dft_gen_fwd",
    )()


def _hy_fwd_kernel(co_ref, sn_ref, w_ref, kre_ref, kim_ref, yre_ref, yim_ref):
    w = w_ref[...]
    tm = co_ref.shape[0]
    pm = min(tm, MM_PIECE_ROWS)
    for p in range(tm // pm):
        rows = slice(p * pm, (p + 1) * pm)
        wre = jnp.dot(co_ref[rows, :], w, preferred_element_type=F32)
        wim = jnp.dot(sn_ref[rows, :], w, preferred_element_type=F32)
        kre, kim = kre_ref[rows, :], kim_ref[rows, :]
        yre_ref[rows, :] = (wre * kre - wim * kim).astype(yre_ref.dtype)
        yim_ref[rows, :] = (wre * kim + wim * kre).astype(yim_ref.dtype)


def _hy_fwd_call(co, sn, w, kre, kim):
    bt, n, width = w.shape
    tm, tn = _tile(n, max(512, HY_DFT_TILE_ELEMS // n)), _tile(width, 512)
    fspec = pl.BlockSpec((tm, n), lambda i, b, j: (i, 0))
    kspec = pl.BlockSpec((tm, tn), lambda i, b, j: (i, j))
    ospec = pl.BlockSpec((None, tm, tn), lambda i, b, j: (b, i, j))
    return pl.pallas_call(
        _hy_fwd_kernel,
        grid=(n // tm, bt, width // tn),
        in_specs=[fspec, fspec, pl.BlockSpec((None, n, tn), lambda i, b, j: (b, 0, j)), kspec, kspec],
        out_specs=[ospec, ospec],
        out_shape=[jax.ShapeDtypeStruct((bt, n, width), BF16)] * 2,
        compiler_params=_params(("parallel", "parallel", "parallel")),
        name="hy_dft_fwd",
    )(co, sn, w, kre, kim)


def _hy_inv_kernel(ct_ref, st_ref, yre_ref, yim_ref, x0_ref, w_ref, bias_ref, o_ref):
    yre, yim = yre_ref[...], yim_ref[...]
    tm = ct_ref.shape[0]
    pm = min(tm, MM_PIECE_ROWS)
    for p in range(tm // pm):
        rows = slice(p * pm, (p + 1) * pm)
        yc = (jnp.dot(ct_ref[rows, :], yre, preferred_element_type=F32)
              + jnp.dot(st_ref[rows, :], yim, preferred_element_type=F32))
        o_ref[rows, :] = (x0_ref[rows, :].astype(F32)
                          * (yc + w_ref[rows, :].astype(F32) * bias_ref[...])).astype(o_ref.dtype)


def _hy_inv_call(ct, st, yre, yim, x0, w, bias):
    bt, n, width = w.shape
    tm, tn = _tile(n, max(512, HY_DFT_TILE_ELEMS // n)), _tile(width, 512)
    fspec = pl.BlockSpec((tm, n), lambda i, b, j: (i, 0))
    yspec = pl.BlockSpec((None, n, tn), lambda i, b, j: (b, 0, j))
    tspec = pl.BlockSpec((None, tm, tn), lambda i, b, j: (b, i, j))
    return pl.pallas_call(
        _hy_inv_kernel,
        grid=(n // tm, bt, width // tn),
        in_specs=[fspec, fspec, yspec, yspec, tspec, tspec, pl.BlockSpec((1, tn), lambda i, b, j: (0, j))],
        out_specs=tspec,
        out_shape=jax.ShapeDtypeStruct((bt, n, width), BF16),
        compiler_params=_params(("parallel", "parallel", "parallel")),
        name="hy_dft_inv",
    )(ct, st, yre, yim, x0, w, bias.reshape(1, width).astype(F32))


def _sgu_kernel(x_ref, lng_ref, lnb_ref, ws_ref, bse_ref, o_ref):
    tl = x_ref.shape[0]
    v = x_ref[:, SG_WIDTH:].astype(F32)
    mu = jnp.mean(v, axis=-1, keepdims=True)
    vc = v - mu
    var = jnp.mean(vc * vc, axis=-1, keepdims=True)
    vn = (vc * lax.rsqrt(var + LN_EPS) * lng_ref[...] + lnb_ref[...]).astype(BF16)
    for g in range(SG_GROUPS):
        cols = slice(g * SG_GROUP_DIM, (g + 1) * SG_GROUP_DIM)
        wsg = ws_ref[g]
        for k in range(tl // SG_CHUNK):
            rows = slice(k * SG_CHUNK, (k + 1) * SG_CHUNK)
            mixed = jnp.dot(wsg, vn[rows, cols], preferred_element_type=F32) + bse_ref[:, cols]
            o_ref[rows, cols] = (x_ref[rows, cols].astype(F32) * mixed).astype(o_ref.dtype)


def _sgu_call(sg_act, ln_g, ln_b, ws, bs):
    bt, seq, _ = sg_act.shape
    tl = _tile(seq, 4 * SG_CHUNK)
    bs_exp = jnp.repeat(bs.astype(F32).T, SG_GROUP_DIM, axis=1)
    return pl.pallas_call(
        _sgu_kernel,
        grid=(bt, seq // tl),
        in_specs=[pl.BlockSpec((None, tl, SG_IN), lambda b, i: (b, i, 0)),
                  pl.BlockSpec((1, SG_WIDTH), lambda b, i: (0, 0)),
                  pl.BlockSpec((1, SG_WIDTH), lambda b, i: (0, 0)),
                  pl.BlockSpec((SG_GROUPS, SG_CHUNK, SG_CHUNK), lambda b, i: (0, 0, 0)),
                  pl.BlockSpec((SG_CHUNK, SG_WIDTH), lambda b, i: (0, 0))],
        out_specs=pl.BlockSpec((None, tl, SG_WIDTH), lambda b, i: (b, i, 0)),
        out_shape=jax.ShapeDtypeStruct((bt, seq, SG_WIDTH), BF16),
        compiler_params=_params(("parallel", "parallel")),
        name="sgu",
    )(sg_act, ln_g.reshape(1, -1).astype(F32), ln_b.reshape(1, -1).astype(F32), ws.astype(BF16), bs_exp)


def _merge_kernel(ym_ref, yh_ref, yg_ref, w0_ref, w1_ref, w2_ref, g0_ref, g1_ref, g2_ref, o_ref):
    dot = functools.partial(jnp.dot, preferred_element_type=F32)
    tm = o_ref.shape[0]
    pm = min(tm, MM_PIECE_ROWS)
    for p in range(tm // pm):
        rows = slice(p * pm, (p + 1) * pm)
        acc = g0_ref[rows, :].astype(F32) * dot(ym_ref[rows, :], w0_ref[0])
        acc = acc + g1_ref[rows, :].astype(F32) * dot(yh_ref[rows, :], w1_ref[0])
        acc = acc + g2_ref[rows, :].astype(F32) * dot(yg_ref[rows, :], w2_ref[0])
        o_ref[rows, :] = acc.astype(o_ref.dtype)


def _merge_call(y_m, y_h, y_g, w_br, layer, gates):
    m = y_m.shape[0]
    tm, tn = _tile(m, 1024), 256
    nb = D_MODEL // tn
    aspec = lambda width: pl.BlockSpec((tm, width), lambda i, j: (i, 0))
    wspec = lambda row0, rows: _wspec(WSlice(w_br, layer, row0, rows, 0), tn, lambda i, j: j)
    gspec = lambda br: pl.BlockSpec((tm, tn), lambda i, j: (i, br * nb + j))
    return pl.pallas_call(
        _merge_kernel,
        grid=(m // tm, nb),
        in_specs=[aspec(SSM_WIDTH), aspec(HY_WIDTH), aspec(SG_WIDTH), wspec(0, SSM_WIDTH),
                  wspec(SSM_WIDTH, HY_WIDTH), wspec(SSM_WIDTH + HY_WIDTH, SG_WIDTH), gspec(0), gspec(1), gspec(2)],
        out_specs=pl.BlockSpec((tm, tn), lambda i, j: (i, j)),
        out_shape=jax.ShapeDtypeStruct((m, D_MODEL), BF16),
        compiler_params=_params(("parallel", "parallel")),
        name="merge",
    )(y_m, y_h, y_g, w_br, w_br, w_br, gates, gates, gates)


def _prep_layer(l, p):
    heads = jnp.arange(2 * SSM_HEADS)[:, None]
    chan_head = (jnp.arange(SSM_WIDTH) // SSM_HEAD_DIM)[None, :]
    return dict(
        layer=l,
        w_in=p["w_in"], w_br=p["w_br"], w_out=p["w_out"], w_up=p["w_up"], w_down=p["w_down"],
        norm1_g=p["norm1_g"][l], norm2_g=p["norm2_g"][l],
        b_gate=p["b_gate"][l].reshape(1, -1).astype(F32),
        ssm_conv_w=p["ssm_conv_w"][l], ssm_conv_b=p["ssm_conv_b"][l],
        dt_bias=p["ssm_dt_bias"][l].reshape(1, -1).astype(F32),
        a_log=p["ssm_a_log"][l].reshape(1, -1).astype(F32),
        e_fwd=(heads == chan_head).astype(BF16), e_bwd=(heads == chan_head + SSM_HEADS).astype(BF16),
        d_exp=jnp.repeat(p["ssm_d"][l].astype(F32), SSM_HEAD_DIM).reshape(1, -1),
        ssm_norm_g=p["ssm_norm_g"][l].reshape(1, -1).astype(F32),
        hy_conv_w=p["hy_conv_w"][l], hy_conv_b=p["hy_conv_b"][l],
        hy_mlp=tuple(p[k][l] for k in ("hy_w1", "hy_b1", "hy_w2", "hy_b2", "hy_w3", "hy_b3", "hy_freq", "hy_w4")),
        hy_bias=p["hy_bias"][l],
        sg_ln_g=p["sg_ln_g"][l], sg_ln_b=p["sg_ln_b"][l], sg_ws=p["sg_ws"][l], sg_bs=p["sg_bs"][l],
        ffn_conv_w=p["ffn_conv_w"][l], ffn_conv_b=p["ffn_conv_b"][l],
    )


W_IN_COLS = dict(z=0, xbc=SSM_WIDTH, dt=SSM_WIDTH + SSM_XBC, hy=SSM_IN, sg=SSM_IN + HY_IN,
                 gate=SSM_IN + HY_IN + SG_IN)


def _w_in(lw, group, extra=0):
    return WSlice(lw["w_in"], lw["layer"], 0, D_MODEL, W_IN_COLS[group] + extra)


def _ssd_branch(lw, h, dt_raw, z_act, bt, seq, tm=1024):
    xbc_act = _mm_conv(h, [(_w_in(lw, "xbc"), lw["ssm_conv_w"], lw["ssm_conv_b"], 0)], seq=seq,
                       width=SSM_XBC, combine=lambda c: (_silu(c),), n_out=1, tm=tm, tn=512,
                       piece_rows=2 * MM_CONV_PIECE_ROWS, name="in_xbc_conv")[0].reshape(bt, seq, SSM_XBC)
    dt3 = dt_raw.reshape(bt, seq, 2 * SSM_HEADS)
    y_fwd = _ssd_call(xbc_act, dt3, lw["dt_bias"], lw["a_log"], lw["e_fwd"], second=False)
    y_m = _ssd_call(xbc_act, dt3, lw["dt_bias"], lw["a_log"], lw["e_bwd"], second=True, y_fwd=y_fwd,
                    z_act=z_act.reshape(bt, seq, SSM_WIDTH), d_exp=lw["d_exp"], norm_g=lw["ssm_norm_g"])
    return y_m.reshape(bt * seq, SSM_WIDTH)


def _hyena_branch(lw, h, dft, bt, seq, tm=1024):
    co, sn, ct, st = dft
    groups = [(_w_in(lw, "hy", off), lw["hy_conv_w"], lw["hy_conv_b"], off) for off in (0, HY_WIDTH, 2 * HY_WIDTH)]
    x0, w = _mm_conv(h, groups, seq=seq, width=HY_WIDTH, combine=lambda c0, c1, c2: (c0, c2 * c1),
                     n_out=2, tm=tm, tn=256, name="in_hy_conv")
    x0 = x0.reshape(bt, seq, HY_WIDTH)
    w = w.reshape(bt, seq, HY_WIDTH)
    ks, kd = _hy_filters(seq, *lw["hy_mlp"])
    kre = _matmul(co, WSlice(ks[None], 0, 0, seq, 0), HY_WIDTH, out_dtype=F32, name="hy_kre", tm=512, tn=512)
    kim = _matmul(sn, WSlice(kd[None], 0, 0, seq, 0), HY_WIDTH, out_dtype=F32, name="hy_kim", tm=512, tn=512)
    yre, yim = _hy_fwd_call(co, sn, w, kre, kim)
    y_h = _hy_inv_call(ct, st, yre, yim, x0, w, lw["hy_bias"])
    return y_h.reshape(bt * seq, HY_WIDTH)


def _layer(x, lw, dft, bt, seq, tm=1024):
    layer = lw["layer"]
    h = _rmsnorm(x, lw["norm1_g"], BF16)
    z_act = _matmul(h, _w_in(lw, "z"), SSM_WIDTH, out_dtype=BF16, name="in_z", epilogue=_ep_silu)
    dt_raw = _matmul(h, _w_in(lw, "dt"), 2 * SSM_HEADS, out_dtype=F32, name="in_dt")
    sg_act = _matmul(h, _w_in(lw, "sg"), SG_IN, out_dtype=BF16, name="in_sg", epilogue=_ep_gelu)
    gates = _matmul(h, _w_in(lw, "gate"), GATE_IN, out_dtype=BF16, name="in_gate", epilogue=_ep_gate,
                    rows=(lw["b_gate"],))
    y_m = _ssd_branch(lw, h, dt_raw, z_act, bt, seq, tm)
    y_h = _hyena_branch(lw, h, dft, bt, seq, tm)
    y_g = _sgu_call(sg_act.reshape(bt, seq, SG_IN), lw["sg_ln_g"], lw["sg_ln_b"], lw["sg_ws"],
                    lw["sg_bs"]).reshape(bt * seq, SG_WIDTH)
    merged = _merge_call(y_m, y_h, y_g, lw["w_br"], layer, gates)
    x = _matmul(merged, WSlice(lw["w_out"], layer, 0, D_MODEL, 0), D_MODEL, out_dtype=F32, name="out_proj",
                epilogue=_ep_residual, tiles=(x,))
    h2 = _rmsnorm(x, lw["norm2_g"], BF16)
    groups = [(WSlice(lw["w_up"], layer, 0, D_MODEL, off), lw["ffn_conv_w"], lw["ffn_conv_b"], off)
              for off in (0, D_FF)]
    act = _mm_conv(h2, groups, seq=seq, width=D_FF, combine=lambda g, v: (_silu(g) * v,), n_out=1,
                   tm=tm, tn=256, piece_rows=2 * MM_CONV_PIECE_ROWS, name="ffn_up_conv")[0]
    return _matmul(act, WSlice(lw["w_down"], layer, 0, D_FF, 0), D_MODEL, out_dtype=F32, name="ffn_down",
                   epilogue=_ep_residual, tiles=(x,), tm=512, tn=512)


def _trunk(x, layers, normf_g):
    bt, seq, d = x.shape
    dft = (*_dft_matrices(seq, inverse=False), *_dft_matrices(seq, inverse=True))
    xf = x.reshape(bt * seq, d)
    for lw in layers:
        xf = _layer(xf, lw, dft, bt, seq)
    return _rmsnorm(xf, normf_g, F32).reshape(bt, seq, d)


def kernel(x_prompt, x_sample, norm1_g, w_in, b_gate, ssm_conv_w, ssm_conv_b, ssm_dt_bias, ssm_a_log, ssm_d,
           ssm_norm_g, hy_conv_w, hy_conv_b, hy_w1, hy_b1, hy_w2, hy_b2, hy_w3, hy_b3, hy_freq, hy_w4, hy_bias,
           sg_ln_g, sg_ln_b, sg_ws, sg_bs, w_br, w_out, norm2_g, w_up, ffn_conv_w, ffn_conv_b, w_down, normf_g):
    bf = lambda w: w.astype(BF16)
    p = dict(norm1_g=norm1_g, w_in=bf(w_in), b_gate=b_gate, ssm_conv_w=ssm_conv_w, ssm_conv_b=ssm_conv_b,
             ssm_dt_bias=ssm_dt_bias, ssm_a_log=ssm_a_log, ssm_d=ssm_d, ssm_norm_g=ssm_norm_g,
             hy_conv_w=hy_conv_w, hy_conv_b=hy_conv_b, hy_w1=hy_w1, hy_b1=hy_b1, hy_w2=hy_w2, hy_b2=hy_b2,
             hy_w3=hy_w3, hy_b3=hy_b3, hy_freq=hy_freq, hy_w4=hy_w4, hy_bias=hy_bias, sg_ln_g=sg_ln_g,
             sg_ln_b=sg_ln_b, sg_ws=sg_ws, sg_bs=sg_bs, w_br=bf(w_br), w_out=bf(w_out), norm2_g=norm2_g,
             w_up=bf(w_up), ffn_conv_w=ffn_conv_w, ffn_conv_b=ffn_conv_b, w_down=bf(w_down))
    layers = [_prep_layer(l, p) for l in range(w_in.shape[0])]
    return (_trunk(x_prompt, layers, normf_g), _trunk(x_sample, layers, normf_g))
```
